```python
import functools
import jax, jax.numpy as jnp
from jax import lax
import numpy as np

D_MODEL = 2048
BATCH = 1
SEQ = 8192
DEPTH = 2
DEC_BATCH = 32
DEC_SEQ = 4
PAST_LEN = 8192
PAGE_SIZE = 128

HEAD_DIM = 128
MIX_W = D_MODEL
W_A = MIX_W // 2
W_B = MIX_W // 4
W_C = MIX_W - W_A - W_B
H_A = W_A // HEAD_DIM
H_C = W_C // HEAD_DIM
CONV_W = 3
CHUNK = 128
Q_BLOCK = 128
D_FF = ((8 * D_MODEL // 3 + 127) // 128) * 128
PLE_DIM = 256
EPS = 1e-6
SB_BIAS_INIT = -8.0
IN_SPLITS = (W_A, 2 * W_A, 3 * W_A, 3 * W_A + W_B, 3 * W_A + 2 * W_B,
             3 * W_A + 3 * W_B, 3 * W_A + 3 * W_B + W_C)
IN_W = 3 * W_A + 3 * W_B + 2 * W_C

kernel_name = "hybrid_sb_conv_chunkmlp_decoder_step"


def rmsnorm(x, g):
    xf = x.astype(jnp.float32)
    y = xf * lax.rsqrt(jnp.mean(xf * xf, axis=-1, keepdims=True) + EPS)
    return (y * g.astype(jnp.float32)).astype(x.dtype)


def head_rmsnorm(x, g):
    shp = x.shape
    xh = x.reshape(*shp[:-1], shp[-1] // HEAD_DIM, HEAD_DIM).astype(jnp.float32)
    y = xh * lax.rsqrt(jnp.mean(xh * xh, axis=-1, keepdims=True) + EPS)
    return (y.reshape(shp) * g.astype(jnp.float32)).astype(x.dtype)


def causal_conv3(x, w, prev):
    xp = jnp.concatenate([prev.astype(x.dtype), x], axis=1)
    T = x.shape[1]
    y = w[0] * xp[:, 0:T]
    for j in range(1, CONV_W):
        y = y + w[j] * xp[:, j:j + T]
    return y.astype(x.dtype), xp[:, -(CONV_W - 1):]


def stick_breaking(q, k, v, bias, q_pos, k_pos):
    z = jnp.einsum('bqhd,bkhd->bhqk', q, k, preferred_element_type=jnp.float32) * (HEAD_DIM ** -0.5)
    z = z + bias.astype(jnp.float32)[None, :, None, None]
    mask = k_pos[None, :] < q_pos[:, None]
    log_fail = jnp.where(mask, -jax.nn.softplus(z), 0.0)
    later = lax.cumsum(log_fail, axis=3, reverse=True) - log_fail
    a = jnp.where(mask, jnp.exp(jax.nn.log_sigmoid(z) + later), 0.0)
    return jnp.einsum('bhqk,bkhd->bqhd', a.astype(v.dtype), v)


def sb_prompt(q, k, v, bias):
    B, S, H, d = q.shape
    nb = S // Q_BLOCK
    qb = jnp.moveaxis(q.reshape(B, nb, Q_BLOCK, H, d), 1, 0)
    k_pos = jnp.arange(S)

    def one_block(args):
        q_blk, start = args
        return stick_breaking(q_blk, k, v, bias, start + jnp.arange(Q_BLOCK), k_pos)

    out = lax.map(one_block, (qb, jnp.arange(nb) * Q_BLOCK))
    return jnp.moveaxis(out, 0, 1).reshape(B, S, H, d)


def sb_sample(q, k, v, bias, past_k, past_v):
    T = q.shape[1]
    P = past_k.shape[1]
    k_all = jnp.concatenate([past_k.astype(k.dtype), k], axis=1)
    v_all = jnp.concatenate([past_v.astype(v.dtype), v], axis=1)
    return stick_breaking(q, k_all, v_all, bias, P + jnp.arange(T), jnp.arange(P + T))


def chunk_mlp(u, v, w_s, b_s):
    B, T, W = v.shape
    c = min(T, CHUNK)
    vr = v.reshape(B, T // c, c, H_C, HEAD_DIM)
    wm = (w_s * jnp.tril(jnp.ones((CHUNK, CHUNK), w_s.dtype)))[:, :c, :c]
    s = jnp.einsum('hts,bnshd->bnthd', wm, vr) + b_s[:, :c].T[None, None, :, :, None]
    return u * s.reshape(B, T, W).astype(u.dtype)


def trunk_layer(h, p, attend, convb_prev, convf_prev,
                g_mix, w_in, b_sb, conv_mix_w, g_v, w_s, b_s, g_out, w_o,
                g_ffn, w_up, conv_ffn_w, w_down, g_ple, w_ple_gate, w_ple_proj):
    B, T, _ = h.shape
    xn = rmsnorm(h, g_mix)
    z = xn @ w_in
    qa, ka, va, bg, cg, hb, uc, vc = jnp.split(z, IN_SPLITS, axis=-1)
    qa = qa.reshape(B, T, H_A, HEAD_DIM)
    ka = ka.reshape(B, T, H_A, HEAD_DIM)
    va = va.reshape(B, T, H_A, HEAD_DIM)
    ya = attend(qa, ka, va, b_sb).reshape(B, T, W_A)
    conv_out, convb_new = causal_conv3(cg * hb, conv_mix_w, convb_prev)
    yb = bg * conv_out
    u = jax.nn.gelu(uc)
    vv = head_rmsnorm(jax.nn.gelu(vc), g_v)
    yc = chunk_mlp(u, vv, w_s, b_s)
    y = head_rmsnorm(jnp.concatenate([ya, yb, yc], axis=-1), g_out)
    h = h + y @ w_o
    up = rmsnorm(h, g_ffn) @ w_up
    upc, convf_new = causal_conv3(up, conv_ffn_w, convf_prev)
    gate, val = jnp.split(upc, 2, axis=-1)
    h = h + (jax.nn.silu(gate) * val) @ w_down
    ple_gate = jax.nn.sigmoid(rmsnorm(h, g_ple) @ w_ple_gate)
    h = h + ple_gate * (p @ w_ple_proj)
    return h, ka, va, convb_new, convf_new, vv


def setup_inputs(seed: int = 0) -> dict:
    key = jax.random.key(seed)
    ks = jax.random.split(key, 32)
    f32 = jnp.float32
    n_pages = PAST_LEN // PAGE_SIZE
    n_used = DEC_BATCH * n_pages
    n_pool = n_used + max(1, n_used // 4)

    def nrm(k, shape, s=1.0):
        return jax.random.normal(k, shape, f32) * s

    def gain(k, shape):
        return 1.0 + 0.02 * jax.random.normal(k, shape, f32)

    page_table = jax.random.permutation(ks[0], n_pool)[:n_used].reshape(DEC_BATCH, n_pages).astype(jnp.int32)
    return {
        "x_prompt": nrm(ks[1], (BATCH, SEQ, D_MODEL)),
        "x_sample": nrm(ks[2], (DEC_BATCH, DEC_SEQ, D_MODEL)),
        "cache_k": nrm(ks[3], (DEPTH, n_pool, PAGE_SIZE, H_A, HEAD_DIM)),
        "cache_v": nrm(ks[4], (DEPTH, n_pool, PAGE_SIZE, H_A, HEAD_DIM)),
        "state_conv_mix": nrm(ks[5], (DEPTH, DEC_BATCH, CONV_W - 1, W_B)),
        "state_conv_ffn": nrm(ks[6], (DEPTH, DEC_BATCH, CONV_W - 1, 2 * D_FF)),
        "page_table": page_table,
        "p_prompt": nrm(ks[7], (DEPTH, BATCH, SEQ, PLE_DIM)),
        "p_sample": nrm(ks[8], (DEPTH, DEC_BATCH, DEC_SEQ, PLE_DIM)),
        "g_mix": gain(ks[9], (DEPTH, D_MODEL)),
        "w_in": nrm(ks[10], (DEPTH, D_MODEL, IN_W), D_MODEL ** -0.5),
        "b_sb": SB_BIAS_INIT + nrm(ks[25], (DEPTH, H_A), 0.1),
        "conv_mix_w": nrm(ks[11], (DEPTH, CONV_W, W_B), CONV_W ** -0.5),
        "g_v": gain(ks[12], (DEPTH, W_C)),
        "w_s": nrm(ks[13], (DEPTH, H_C, CHUNK, CHUNK), CHUNK ** -0.5),
        "b_s": nrm(ks[14], (DEPTH, H_C, CHUNK), 0.02),
        "g_out": gain(ks[15], (DEPTH, MIX_W)),
        "w_o": nrm(ks[16], (DEPTH, MIX_W, D_MODEL), MIX_W ** -0.5),
        "g_ffn": gain(ks[17], (DEPTH, D_MODEL)),
        "w_up": nrm(ks[18], (DEPTH, D_MODEL, 2 * D_FF), D_MODEL ** -0.5),
        "conv_ffn_w": nrm(ks[19], (DEPTH, CONV_W, 2 * D_FF), CONV_W ** -0.5),
        "w_down": nrm(ks[20], (DEPTH, D_FF, D_MODEL), D_FF ** -0.5),
        "g_ple": gain(ks[21], (DEPTH, D_MODEL)),
        "w_ple_gate": nrm(ks[22], (DEPTH, D_MODEL, D_MODEL), D_MODEL ** -0.5),
        "w_ple_proj": nrm(ks[23], (DEPTH, PLE_DIM, D_MODEL), PLE_DIM ** -0.5),
        "g_final": gain(ks[24], (D_MODEL,)),
    }


def reference(x_prompt, x_sample, cache_k, cache_v, state_conv_mix, state_conv_ffn, page_table,
              p_prompt, p_sample, g_mix, w_in, b_sb, conv_mix_w, g_v, w_s, b_s, g_out, w_o,
              g_ffn, w_up, conv_ffn_w, w_down, g_ple, w_ple_gate, w_ple_proj, g_final):
    n_seq, n_pages = page_table.shape
    past_len = n_pages * cache_k.shape[2]
    bp = x_prompt.shape[0]
    hp = x_prompt
    hs = x_sample
    kp_l, vp_l, cbp_l, cfp_l = [], [], [], []
    ks_l, vs_l, cbs_l, cfs_l, cvs_l = [], [], [], [], []
    for i in range(DEPTH):
        lw = (g_mix[i], w_in[i], b_sb[i], conv_mix_w[i], g_v[i], w_s[i], b_s[i], g_out[i], w_o[i],
              g_ffn[i], w_up[i], conv_ffn_w[i], w_down[i], g_ple[i], w_ple_gate[i], w_ple_proj[i])
        hp, kp, vp, cbp, cfp, _ = trunk_layer(
            hp, p_prompt[i], sb_prompt,
            jnp.zeros((bp, CONV_W - 1, W_B), hp.dtype),
            jnp.zeros((bp, CONV_W - 1, 2 * D_FF), hp.dtype), *lw)
        kp_l.append(kp); vp_l.append(vp); cbp_l.append(cbp); cfp_l.append(cfp)
        past_k = cache_k[i][page_table].reshape(n_seq, past_len, H_A, HEAD_DIM)
        past_v = cache_v[i][page_table].reshape(n_seq, past_len, H_A, HEAD_DIM)
        attend_s = functools.partial(sb_sample, past_k=past_k, past_v=past_v)
        hs, ksm, vsm, cbs, cfs, cvs = trunk_layer(
            hs, p_sample[i], attend_s, state_conv_mix[i], state_conv_ffn[i], *lw)
        ks_l.append(ksm); vs_l.append(vsm); cbs_l.append(cbs); cfs_l.append(cfs); cvs_l.append(cvs)
    y_prompt = rmsnorm(hp, g_final)
    y_sample = rmsnorm(hs, g_final)
    return (y_prompt, y_sample,
            jnp.stack(kp_l), jnp.stack(vp_l), jnp.stack(cbp_l), jnp.stack(cfp_l),
            jnp.stack(ks_l), jnp.stack(vs_l), jnp.stack(cbs_l), jnp.stack(cfs_l), jnp.stack(cvs_l))
```

```python
import functools

import numpy as np
import jax
import jax.numpy as jnp
from jax import lax
from jax.experimental import pallas as pl
from jax.experimental.pallas import tpu as pltpu

F32 = jnp.float32
BF16 = jnp.bfloat16

D_MODEL = 2048
HEAD_DIM = 128
W_A = 1024
W_B = 512
W_C = 512
H_A = W_A // HEAD_DIM
H_C = W_C // HEAD_DIM
N_HEADS = D_MODEL // HEAD_DIM
IN_W = 3 * W_A + 3 * W_B + 2 * W_C
REST_W = IN_W - 3 * W_A
D_FF = 5504
CHUNK = 128
PAGE = 128
PLE_DIM = 256
EPS = 1e-6
SCALE = HEAD_DIM ** -0.5

LANE = 128
SUBLANE = 8
FF_TILE = 512
D_FF_PAD = ((D_FF + FF_TILE - 1) // FF_TILE) * FF_TILE
IN_TILE = 512
VMEM_LIMIT = 56 * 1024 * 1024


def _cparams(*sem):
    return pltpu.CompilerParams(dimension_semantics=sem, vmem_limit_bytes=VMEM_LIMIT)


def _rms_rows(x, g):
    ms = jnp.mean(x * x, axis=-1, keepdims=True)
    return x * lax.rsqrt(ms + EPS) * g


def _gelu_tanh(x):
    c = np.float32(np.sqrt(2.0 / np.pi))
    return 0.5 * x * (1.0 + jnp.tanh(c * (x + 0.044715 * (x * x * x))))


def _softplus(z):
    return jnp.maximum(z, 0.0) + jnp.log1p(jnp.exp(-jnp.abs(z)))


def _split_bf16(x):
    hi = x.astype(BF16)
    lo = (x - hi.astype(F32)).astype(BF16)
    return hi, lo


def _in_proj_kernel(x_ref, g_ref, w_ref, qkv_ref, k_ref, v_ref, rest_ref, xn_ref):
    j = pl.program_id(1)

    @pl.when(j == 0)
    def _():
        xn_ref[...] = _rms_rows(x_ref[...], g_ref[...]).astype(BF16)

    r = jnp.dot(xn_ref[...], w_ref[...], preferred_element_type=F32)
    nq = W_A // IN_TILE

    @pl.when(j < nq)
    def _():
        qkv_ref[...] = (r * SCALE).astype(BF16)

    @pl.when(jnp.logical_and(j >= nq, j < 3 * nq))
    def _():
        qkv_ref[...] = r.astype(BF16)

    @pl.when(jnp.logical_and(j >= nq, j < 2 * nq))
    def _():
        k_ref[...] = r

    @pl.when(jnp.logical_and(j >= 2 * nq, j < 3 * nq))
    def _():
        v_ref[...] = r

    @pl.when(j >= 3 * nq)
    def _():
        rest_ref[...] = r


def _in_proj(x, g, w, tm):
    m = x.shape[0]
    nq = W_A // IN_TILE
    nj = IN_W // IN_TILE
    nrest = REST_W // IN_TILE
    return pl.pallas_call(
        _in_proj_kernel,
        grid=(m // tm, nj),
        in_specs=[
            pl.BlockSpec((tm, D_MODEL), lambda i, j: (i, 0)),
            pl.BlockSpec((1, D_MODEL), lambda i, j: (0, 0)),
            pl.BlockSpec((D_MODEL, IN_TILE), lambda i, j: (0, j)),
        ],
        out_specs=[
            pl.BlockSpec((tm, IN_TILE), lambda i, j: (i, jnp.minimum(j, 3 * nq - 1))),
            pl.BlockSpec((tm, IN_TILE), lambda i, j: (i, jnp.clip(j - nq, 0, nq - 1))),
            pl.BlockSpec((tm, IN_TILE), lambda i, j: (i, jnp.clip(j - 2 * nq, 0, nq - 1))),
            pl.BlockSpec((tm, IN_TILE), lambda i, j: (i, jnp.clip(j - 3 * nq, 0, nrest - 1))),
        ],
        out_shape=[
            jax.ShapeDtypeStruct((m, 3 * W_A), BF16),
            jax.ShapeDtypeStruct((m, W_A), F32),
            jax.ShapeDtypeStruct((m, W_A), F32),
            jax.ShapeDtypeStruct((m, REST_W), F32),
        ],
        scratch_shapes=[pltpu.VMEM((tm, D_MODEL), BF16)],
        compiler_params=_cparams("arbitrary", "arbitrary"),
        name="in_proj",
    )(x, g, w)


ATT_TQ = 256
ATT_TK = 128


def _cumsum_matrix():
    j = np.arange(ATT_TK)[:, None]
    s = np.arange(ATT_TK)[None, :]
    u = (j > s).astype(np.float32)
    blk = np.concatenate([u, np.ones((ATT_TK, ATT_TK), np.float32)], axis=1)
    return jnp.asarray(np.concatenate([blk, blk], axis=0), dtype=BF16)


def _attn_kernel(b_ref, q_ref, k_ref, v_ref, u_ref, o_ref, acc_ref, carry_ref):
    h = pl.program_id(0)
    i = pl.program_id(1)
    bias = b_ref[h]
    q = q_ref[...]
    acc_ref[...] = jnp.zeros_like(acc_ref)
    carry_ref[...] = jnp.zeros_like(carry_ref)

    def tile(kj, masked):
        start = pl.multiple_of(kj * ATT_TK, ATT_TK)
        k = k_ref[pl.ds(start, ATT_TK), :]
        v = v_ref[pl.ds(start, ATT_TK), :]
        z = lax.dot_general(q, k, (((1,), (1,)), ((), ())), preferred_element_type=F32) + bias
        lf = -_softplus(z)
        if masked:
            row = lax.broadcasted_iota(jnp.int32, (ATT_TQ, ATT_TK), 0) + i * ATT_TQ
            col = lax.broadcasted_iota(jnp.int32, (ATT_TQ, ATT_TK), 1) + kj * ATT_TK
            valid = col < row
            lf = jnp.where(valid, lf, 0.0)
        hi, lo = _split_bf16(lf)
        r = jnp.dot(jnp.concatenate([hi, lo], axis=1), u_ref[...], preferred_element_type=F32)
        a = jnp.exp(z + lf + r[:, :ATT_TK] + carry_ref[...])
        if masked:
            a = jnp.where(valid, a, 0.0)
        acc_ref[...] += jnp.dot(a.astype(BF16), v, preferred_element_type=F32)
        carry_ref[...] += r[:, ATT_TK:]

    per = ATT_TQ // ATT_TK
    for d in range(per - 1, -1, -1):
        tile(i * per + d, True)

    def body(it, c):
        span = i - 1 - it
        for d in range(per - 1, -1, -1):
            tile(span * per + d, False)
        return c

    lax.fori_loop(0, i, body, 0)
    o_ref[...] = acc_ref[...]


def _attention_prompt(qkv, b_sb):
    s = qkv.shape[0]
    return pl.pallas_call(
        _attn_kernel,
        grid=(H_A, s // ATT_TQ),
        in_specs=[
            pl.BlockSpec(memory_space=pltpu.SMEM),
            pl.BlockSpec((ATT_TQ, HEAD_DIM), lambda h, i: (i, h)),
            pl.BlockSpec((s, HEAD_DIM), lambda h, i: (0, H_A + h)),
            pl.BlockSpec((s, HEAD_DIM), lambda h, i: (0, 2 * H_A + h)),
            pl.BlockSpec((2 * ATT_TK, 2 * ATT_TK), lambda h, i: (0, 0)),
        ],
        out_specs=pl.BlockSpec((ATT_TQ, HEAD_DIM), lambda h, i: (i, h)),
        out_shape=jax.ShapeDtypeStruct((s, W_A), F32),
        scratch_shapes=[pltpu.VMEM((ATT_TQ, HEAD_DIM), F32), pltpu.VMEM((ATT_TQ, ATT_TK), F32)],
        compiler_params=_cparams("arbitrary", "arbitrary"),
        name="attn_prompt",
    )(b_sb, qkv, qkv, qkv, _cumsum_matrix())


SCOLS = LANE


def _sattn_kernel(pt_ref, qbd_ref, knew_ref, vnew_ref, kp_ref, vp_ref, brow_ref, nmask_ref,
                  l_ref, hm_ref, o_ref, acc_ref, carry_ref, *, n_q):
    del pt_ref
    p = pl.program_id(1)
    qbd = qbd_ref[...]
    brow = brow_ref[...]
    lmat = l_ref[...]
    nrow = n_q * H_A

    def process(kb, vb, mask):
        zt = jnp.dot(kb, qbd, preferred_element_type=F32) + brow
        lf = -_softplus(zt)
        if mask is not None:
            lf = jnp.where(mask > 0.0, lf, 0.0)
        hi, lo = _split_bf16(lf)
        later = (jnp.dot(lmat, hi, preferred_element_type=F32)
                 + jnp.dot(lmat, lo, preferred_element_type=F32))
        a = jnp.exp(zt + lf + later + carry_ref[...])
        if mask is not None:
            a = jnp.where(mask > 0.0, a, 0.0)
        at = a.T[:nrow].astype(BF16)
        acc_ref[...] += jnp.dot(at, vb, preferred_element_type=F32)
        carry_ref[...] += jnp.sum(lf, axis=0, keepdims=True)

    @pl.when(p == 0)
    def _():
        acc_ref[...] = jnp.zeros_like(acc_ref)
        carry_ref[...] = jnp.zeros_like(carry_ref)
        process(knew_ref[...], vnew_ref[...], nmask_ref[...])

    process(kp_ref[...].astype(BF16), vp_ref[...].astype(BF16), None)

    @pl.when(p == pl.num_programs(1) - 1)
    def _():
        res = acc_ref[...] * hm_ref[...]
        o_ref[...] = jnp.sum(res.reshape(n_q, H_A, W_A), axis=1)


def _attention_sample(qkv, k_new, v_new, b_sb, cache_k, cache_v, page_table, layer):
    n_seq, n_pages = page_table.shape
    n_q = qkv.shape[0] // n_seq
    nrow = n_q * H_A
    q = qkv[:, :W_A].reshape(n_seq, n_q, H_A, HEAD_DIM)
    eye = jnp.eye(H_A, dtype=BF16)
    qbd = jnp.einsum("bthd,hg->bhdtg", q, eye).reshape(n_seq, W_A, nrow)
    qbd = jnp.pad(qbd, ((0, 0), (0, 0), (0, SCOLS - nrow)))
    pad_rows = ((0, 0), (0, PAGE - n_q), (0, 0))
    knew = jnp.pad(k_new.reshape(n_seq, n_q, W_A), pad_rows).astype(BF16)
    vnew = jnp.pad(v_new.reshape(n_seq, n_q, W_A), pad_rows).astype(BF16)
    cols = np.arange(SCOLS)
    brow = jnp.where(cols < nrow, b_sb[cols % H_A], 0.0).reshape(1, SCOLS).astype(F32)
    keys = np.arange(PAGE)[:, None]
    nmask = jnp.asarray(((cols[None, :] < nrow) & (keys < cols[None, :] // H_A)).astype(np.float32))
    lmat = jnp.asarray((np.arange(PAGE)[None, :] > keys).astype(np.float32), dtype=BF16)
    hm = jnp.asarray((np.arange(W_A)[None, :] // HEAD_DIM
                      == np.arange(nrow)[:, None] % H_A).astype(np.float32))
    n_pool = cache_k.shape[1]
    ck = cache_k.reshape(cache_k.shape[0], n_pool, PAGE, W_A)
    cv = cache_v.reshape(cache_v.shape[0], n_pool, PAGE, W_A)

    def page_map(b, p, pt):
        return (layer, pt[b, n_pages - 1 - p], 0, 0)

    grid_spec = pltpu.PrefetchScalarGridSpec(
        num_scalar_prefetch=1,
        grid=(n_seq, n_pages),
        in_specs=[
            pl.BlockSpec((None, W_A, SCOLS), lambda b, p, pt: (b, 0, 0)),
            pl.BlockSpec((None, PAGE, W_A), lambda b, p, pt: (b, 0, 0)),
            pl.BlockSpec((None, PAGE, W_A), lambda b, p, pt: (b, 0, 0)),
            pl.BlockSpec((None, None, PAGE, W_A), page_map),
            pl.BlockSpec((None, None, PAGE, W_A), page_map),
            pl.BlockSpec((1, SCOLS), lambda b, p, pt: (0, 0)),
            pl.BlockSpec((PAGE, SCOLS), lambda b, p, pt: (0, 0)),
            pl.BlockSpec((PAGE, PAGE), lambda b, p, pt: (0, 0)),
            pl.BlockSpec((nrow, W_A), lambda b, p, pt: (0, 0)),
        ],
        out_specs=pl.BlockSpec((None, n_q, W_A), lambda b, p, pt: (b, 0, 0)),
        scratch_shapes=[pltpu.VMEM((nrow, W_A), F32), pltpu.VMEM((1, SCOLS), F32)],
    )
    out = pl.pallas_call(
        functools.partial(_sattn_kernel, n_q=n_q),
        grid_spec=grid_spec,
        out_shape=jax.ShapeDtypeStruct((n_seq, n_q, W_A), F32),
        compiler_params=_cparams("arbitrary", "arbitrary"),
        name="attn_sample",
    )(page_table, qbd, knew, vnew, ck, cv, brow, nmask, lmat, hm)
    return out.reshape(n_seq * n_q, W_A)


def _conv3_rows(x, cw_ref, buf_ref, prev8, p1, p2, seq_len):
    tm = x.shape[0]
    buf_ref[pl.ds(0, SUBLANE), :] = prev8
    buf_ref[pl.ds(SUBLANE, tm), :] = x
    x1 = buf_ref[pl.ds(SUBLANE - 1, tm), :]
    x2 = buf_ref[pl.ds(SUBLANE - 2, tm), :]
    if p1 is not None:
        t = lax.broadcasted_iota(jnp.int32, x.shape, 0) % seq_len
        x1 = jnp.where(t == 0, p1, x1)
        x2 = jnp.where(t < 2, p2, x2)
    cw = cw_ref[...]
    return cw[0:1, :] * x2 + cw[1:2, :] * x1 + cw[2:3, :] * x


def _mix_kernel(*refs, sample, seq_len):
    if sample:
        (h_ref, ya_ref, rest_ref, cw_ref, gv_ref, ws_ref, bs_ref, go_ref, wo_ref, p1_ref, p2_ref,
         out_ref, x_ref, vv_ref, buf_ref, y_ref) = refs
    else:
        (h_ref, ya_ref, rest_ref, cw_ref, gv_ref, ws_ref, bs_ref, go_ref, wo_ref,
         out_ref, x_ref, carry_ref, buf_ref, y_ref) = refs
    tm = h_ref.shape[0]
    i = pl.program_id(0)

    bg = rest_ref[:, 0:W_B]
    x = rest_ref[:, W_B:2 * W_B] * rest_ref[:, 2 * W_B:3 * W_B]
    x_ref[...] = x
    if sample:
        conv = _conv3_rows(x, cw_ref, buf_ref, jnp.zeros((SUBLANE, W_B), F32),
                           p1_ref[...], p2_ref[...], seq_len)
    else:
        @pl.when(i == 0)
        def _():
            carry_ref[...] = jnp.zeros_like(carry_ref)
        conv = _conv3_rows(x, cw_ref, buf_ref, carry_ref[...], None, None, seq_len)
        carry_ref[...] = x[tm - SUBLANE:, :]
    yb = bg * conv

    go = go_ref[...]

    def put_head(hidx, yh):
        ms = jnp.mean(yh * yh, axis=-1, keepdims=True)
        lo = hidx * HEAD_DIM
        y_ref[:, lo:lo + HEAD_DIM] = (yh * lax.rsqrt(ms + EPS) * go[:, lo:lo + HEAD_DIM]).astype(BF16)

    for hh in range(H_A):
        put_head(hh, ya_ref[:, hh * HEAD_DIM:(hh + 1) * HEAD_DIM])
    for hh in range(W_B // HEAD_DIM):
        put_head(H_A + hh, yb[:, hh * HEAD_DIM:(hh + 1) * HEAD_DIM])

    gv = gv_ref[...]
    bs = bs_ref[...]
    for hh in range(H_C):
        lo = hh * HEAD_DIM
        u = _gelu_tanh(rest_ref[:, 3 * W_B + lo:3 * W_B + lo + HEAD_DIM])
        gvc = _gelu_tanh(rest_ref[:, 3 * W_B + W_C + lo:3 * W_B + W_C + lo + HEAD_DIM])
        ms = jnp.mean(gvc * gvc, axis=-1, keepdims=True)
        vv = gvc * lax.rsqrt(ms + EPS) * gv[:, lo:lo + HEAD_DIM]
        if sample:
            vv_ref[:, lo:lo + HEAD_DIM] = vv
        vvb = vv.astype(BF16)
        w = ws_ref[hh]
        parts = []
        for c in range(tm // CHUNK):
            parts.append(jnp.dot(w, vvb[c * CHUNK:(c + 1) * CHUNK, :], preferred_element_type=F32)
                         + bs[:, lo:lo + HEAD_DIM])
        s = parts[0] if len(parts) == 1 else jnp.concatenate(parts, axis=0)
        put_head(H_A + W_B // HEAD_DIM + hh, u * s)

    out_ref[...] = h_ref[...] + jnp.dot(y_ref[...], wo_ref[...], preferred_element_type=F32)


def _mix(h, ya, rest, cw, g_v, ws, bs, g_out, w_o, tm, p1=None, p2=None, seq_len=1):
    m = h.shape[0]
    sample = p1 is not None
    row = lambda i: (i, 0)
    fixed = lambda i: (0, 0)
    in_specs = [
        pl.BlockSpec((tm, D_MODEL), row),
        pl.BlockSpec((tm, W_A), row),
        pl.BlockSpec((tm, REST_W), row),
        pl.BlockSpec((3, W_B), fixed),
        pl.BlockSpec((1, W_C), fixed),
        pl.BlockSpec((H_C, CHUNK, CHUNK), lambda i: (0, 0, 0)),
        pl.BlockSpec((CHUNK, W_C), fixed),
        pl.BlockSpec((1, D_MODEL), fixed),
        pl.BlockSpec((D_MODEL, D_MODEL), fixed),
    ]
    args = [h, ya, rest, cw, g_v, ws, bs, g_out, w_o]
    out_specs = [pl.BlockSpec((tm, D_MODEL), row), pl.BlockSpec((tm, W_B), row)]
    out_shape = [jax.ShapeDtypeStruct((m, D_MODEL), F32), jax.ShapeDtypeStruct((m, W_B), F32)]
    scratch = []
    if sample:
        in_specs += [pl.BlockSpec((tm, W_B), row), pl.BlockSpec((tm, W_B), row)]
        args += [p1, p2]
        out_specs.append(pl.BlockSpec((tm, W_C), row))
        out_shape.append(jax.ShapeDtypeStruct((m, W_C), F32))
    else:
        scratch.append(pltpu.VMEM((SUBLANE, W_B), F32))
    scratch += [pltpu.VMEM((tm + SUBLANE, W_B), F32), pltpu.VMEM((tm, D_MODEL), BF16)]
    return pl.pallas_call(
        functools.partial(_mix_kernel, sample=sample, seq_len=seq_len),
        grid=(m // tm,),
        in_specs=in_specs,
        out_specs=out_specs,
        out_shape=out_shape,
        scratch_shapes=scratch,
        compiler_params=_cparams("arbitrary"),
        name="mix_sample" if sample else "mix_prompt",
    )(*args)


def _ffn_kernel(*refs, sample, seq_len):
    if sample:
        (h_ref, g_ref, wg_ref, wv_ref, cwg_ref, cwv_ref, wd_ref, p1g_ref, p2g_ref, p1v_ref, p2v_ref,
         out_ref, upg_ref, upv_ref, xn_ref, acc_ref, buf_ref) = refs
    else:
        (h_ref, g_ref, wg_ref, wv_ref, cwg_ref, cwv_ref, wd_ref,
         out_ref, upg_ref, upv_ref, xn_ref, acc_ref, buf_ref, carry_ref) = refs
    tm = h_ref.shape[0]
    i = pl.program_id(0)
    j = pl.program_id(1)

    @pl.when(j == 0)
    def _():
        xn_ref[...] = _rms_rows(h_ref[...], g_ref[...]).astype(BF16)
        acc_ref[...] = jnp.zeros_like(acc_ref)

    if not sample:
        @pl.when(jnp.logical_and(i == 0, j == 0))
        def _():
            carry_ref[...] = jnp.zeros_like(carry_ref)

    def branch(w_ref, cw_ref, idx, up_ref, p1_ref, p2_ref):
        up = jnp.dot(xn_ref[...], w_ref[...], preferred_element_type=F32)
        if sample:
            up_ref[...] = up
            return _conv3_rows(up, cw_ref, buf_ref, jnp.zeros((SUBLANE, FF_TILE), F32),
                               p1_ref[...], p2_ref[...], seq_len)
        conv = _conv3_rows(up, cw_ref, buf_ref, carry_ref[idx, j], None, None, seq_len)
        last = up[tm - SUBLANE:, :]
        carry_ref[idx, j] = last
        up_ref[...] = last
        return conv

    if sample:
        gc = branch(wg_ref, cwg_ref, 0, upg_ref, p1g_ref, p2g_ref)
        vc = branch(wv_ref, cwv_ref, 1, upv_ref, p1v_ref, p2v_ref)
    else:
        gc = branch(wg_ref, cwg_ref, 0, upg_ref, None, None)
        vc = branch(wv_ref, cwv_ref, 1, upv_ref, None, None)
    act = gc * (1.0 / (1.0 + jnp.exp(-gc))) * vc
    acc_ref[...] += jnp.dot(act.astype(BF16), wd_ref[...], preferred_element_type=F32)

    @pl.when(j == pl.num_programs(1) - 1)
    def _():
        out_ref[...] = h_ref[...] + acc_ref[...]


def _ffn(h, g, wg, wv, cwg, cwv, wd, tm, prevs=None, seq_len=1):
    m = h.shape[0]
    sample = prevs is not None
    nj = D_FF_PAD // FF_TILE
    row = lambda i, j: (i, 0)
    col = lambda i, j: (0, j)
    in_specs = [
        pl.BlockSpec((tm, D_MODEL), row),
        pl.BlockSpec((1, D_MODEL), lambda i, j: (0, 0)),
        pl.BlockSpec((D_MODEL, FF_TILE), col),
        pl.BlockSpec((D_MODEL, FF_TILE), col),
        pl.BlockSpec((3, FF_TILE), col),
        pl.BlockSpec((3, FF_TILE), col),
        pl.BlockSpec((FF_TILE, D_MODEL), lambda i, j: (j, 0)),
    ]
    args = [h, g, wg, wv, cwg, cwv, wd]
    up_rows = tm if sample else SUBLANE
    up_spec = pl.BlockSpec((up_rows, FF_TILE), lambda i, j: (i, j))
    up_shape = jax.ShapeDtypeStruct((m // tm * up_rows, D_FF_PAD), F32)
    out_specs = [pl.BlockSpec((tm, D_MODEL), row), up_spec, up_spec]
    out_shape = [jax.ShapeDtypeStruct((m, D_MODEL), F32), up_shape, up_shape]
    scratch = [pltpu.VMEM((tm, D_MODEL), BF16), pltpu.VMEM((tm, D_MODEL), F32),
               pltpu.VMEM((tm + SUBLANE, FF_TILE), F32)]
    if sample:
        in_specs += [pl.BlockSpec((tm, FF_TILE), col)] * 4
        args += list(prevs)
    else:
        scratch.append(pltpu.VMEM((2, nj, SUBLANE, FF_TILE), F32))
    return pl.pallas_call(
        functools.partial(_ffn_kernel, sample=sample, seq_len=seq_len),
        grid=(m // tm, nj),
        in_specs=in_specs,
        out_specs=out_specs,
        out_shape=out_shape,
        scratch_shapes=scratch,
        compiler_params=_cparams("arbitrary", "arbitrary"),
        name="ffn_sample" if sample else "ffn_prompt",
    )(*args)


def _ple_kernel(h_ref, g_ref, p_ref, wpg_ref, wpp_ref, gf_ref, out_ref, *, final):
    h = h_ref[...]
    xn = _rms_rows(h, g_ref[...]).astype(BF16)
    gate = jnp.dot(xn, wpg_ref[...], preferred_element_type=F32)
    gate = 1.0 / (1.0 + jnp.exp(-gate))
    proj = jnp.dot(p_ref[...].astype(BF16), wpp_ref[...], preferred_element_type=F32)
    hn = h + gate * proj
    if final:
        hn = _rms_rows(hn, gf_ref[...])
    out_ref[...] = hn


def _ple(h, g, p, wpg, wpp, g_final, tm, final):
    m = h.shape[0]
    row = lambda i: (i, 0)
    fixed = lambda i: (0, 0)
    return pl.pallas_call(
        functools.partial(_ple_kernel, final=final),
        grid=(m // tm,),
        in_specs=[
            pl.BlockSpec((tm, D_MODEL), row),
            pl.BlockSpec((1, D_MODEL), fixed),
            pl.BlockSpec((tm, PLE_DIM), row),
            pl.BlockSpec((D_MODEL, D_MODEL), fixed),
            pl.BlockSpec((PLE_DIM, D_MODEL), fixed),
            pl.BlockSpec((1, D_MODEL), fixed),
        ],
        out_specs=pl.BlockSpec((tm, D_MODEL), row),
        out_shape=jax.ShapeDtypeStruct((m, D_MODEL), F32),
        compiler_params=_cparams("arbitrary"),
        name="ple",
    )(h, g, p, wpg, wpp, g_final)


def _expand_state(state, seq_len):
    b, _, c = state.shape
    zeros = jnp.zeros((b, seq_len - 1, c), state.dtype)
    p1 = jnp.concatenate([state[:, 1:2], zeros], axis=1).reshape(b * seq_len, c)
    p2 = jnp.concatenate([state[:, 0:1], state[:, 1:2], zeros[:, 1:]], axis=1).reshape(b * seq_len, c)
    return p1, p2


def _pad_ff(x):
    return jnp.pad(x, ((0, 0), (0, D_FF_PAD - D_FF)))


def kernel(x_prompt, x_sample, cache_k, cache_v, state_conv_mix, state_conv_ffn, page_table,
           p_prompt, p_sample, g_mix, w_in, b_sb, conv_mix_w, g_v, w_s, b_s, g_out, w_o,
           g_ffn, w_up, conv_ffn_w, w_down, g_ple, w_ple_gate, w_ple_proj, g_final):
    depth = w_in.shape[0]
    bp, seq, _ = x_prompt.shape
    n_seq, n_q, _ = x_sample.shape
    assert bp == 1 and n_q >= 2 and n_q * n_seq == CHUNK
    ms = n_seq * n_q
    hp = x_prompt.reshape(seq, D_MODEL)
    hs = x_sample.reshape(ms, D_MODEL)
    gfin = g_final.reshape(1, D_MODEL)
    tril = jnp.tril(jnp.ones((CHUNK, CHUNK), F32))

    outs = {k: [] for k in ("kp", "vp", "cbp", "cfp", "ks", "vs", "cbs", "cfs", "cvs")}
    for l in range(depth):
        w_in_b = w_in[l].astype(BF16)
        w_o_b = w_o[l].astype(BF16)
        wg_b = _pad_ff(w_up[l][:, :D_FF]).astype(BF16)
        wv_b = _pad_ff(w_up[l][:, D_FF:]).astype(BF16)
        cwg = _pad_ff(conv_ffn_w[l][:, :D_FF])
        cwv = _pad_ff(conv_ffn_w[l][:, D_FF:])
        wd_b = jnp.pad(w_down[l], ((0, D_FF_PAD - D_FF), (0, 0))).astype(BF16)
        wpg_b = w_ple_gate[l].astype(BF16)
        wpp_b = w_ple_proj[l].astype(BF16)
        gm = g_mix[l].reshape(1, D_MODEL)
        gv = g_v[l].reshape(1, W_C)
        go = g_out[l].reshape(1, D_MODEL)
        gf = g_ffn[l].reshape(1, D_MODEL)
        gp = g_ple[l].reshape(1, D_MODEL)
        final = l == depth - 1

        ws_p = (w_s[l] * tril).astype(BF16)
        bs_p = jnp.repeat(b_s[l].T, HEAD_DIM, axis=1)
        wm_q = (w_s[l] * tril)[:, :n_q, :n_q]
        ws_s = jnp.einsum("ab,hts->hatbs", jnp.eye(n_seq, dtype=F32), wm_q).reshape(H_C, ms, ms).astype(BF16)
        bs_s = jnp.repeat(jnp.tile(b_s[l][:, :n_q].T, (n_seq, 1)), HEAD_DIM, axis=1)

        qkv, kp, vp, rest = _in_proj(hp, gm, w_in_b, 1024)
        ya = _attention_prompt(qkv, b_sb[l])
        hp, xcp = _mix(hp, ya, rest, conv_mix_w[l], gv, ws_p, bs_p, go, w_o_b, 256)
        hp, cfg, cfv = _ffn(hp, gf, wg_b, wv_b, cwg, cwv, wd_b, 512)
        hp = _ple(hp, gp, p_prompt[l].reshape(seq, PLE_DIM), wpg_b, wpp_b, gfin, 512, final)
        outs["kp"].append(kp.reshape(1, seq, H_A, HEAD_DIM))
        outs["vp"].append(vp.reshape(1, seq, H_A, HEAD_DIM))
        outs["cbp"].append(xcp[seq - 2:].reshape(1, 2, W_B))
        outs["cfp"].append(jnp.concatenate([cfg[-2:, :D_FF], cfv[-2:, :D_FF]],
                                           axis=-1).reshape(1, 2, 2 * D_FF))

        qkv_s, ks, vs, rest_s = _in_proj(hs, gm, w_in_b, ms)
        ya_s = _attention_sample(qkv_s, ks, vs, b_sb[l], cache_k, cache_v, page_table, l)
        p1m, p2m = _expand_state(state_conv_mix[l], n_q)
        hs, xcs, vvs = _mix(hs, ya_s, rest_s, conv_mix_w[l], gv, ws_s, bs_s, go, w_o_b, ms,
                            p1=p1m, p2=p2m, seq_len=n_q)
        p1g, p2g = _expand_state(_pad_ff(state_conv_ffn[l][..., :D_FF].reshape(n_seq * 2, D_FF))
                                 .reshape(n_seq, 2, D_FF_PAD), n_q)
        p1v, p2v = _expand_state(_pad_ff(state_conv_ffn[l][..., D_FF:].reshape(n_seq * 2, D_FF))
                                 .reshape(n_seq, 2, D_FF_PAD), n_q)
        hs, upg, upv = _ffn(hs, gf, wg_b, wv_b, cwg, cwv, wd_b, ms,
                            prevs=(p1g, p2g, p1v, p2v), seq_len=n_q)
        hs = _ple(hs, gp, p_sample[l].reshape(ms, PLE_DIM), wpg_b, wpp_b, gfin, ms, final)
        outs["ks"].append(ks.reshape(n_seq, n_q, H_A, HEAD_DIM))
        outs["vs"].append(vs.reshape(n_seq, n_q, H_A, HEAD_DIM))
        outs["cbs"].append(xcs.reshape(n_seq, n_q, W_B)[:, n_q - 2:])
        up_s = jnp.concatenate([upg[:, :D_FF], upv[:, :D_FF]], axis=-1)
        outs["cfs"].append(up_s.reshape(n_seq, n_q, 2 * D_FF)[:, n_q - 2:])
        outs["cvs"].append(vvs.reshape(n_seq, n_q, W_C))

    st = lambda k: jnp.stack(outs[k])
    return (hp.reshape(1, seq, D_MODEL), hs.reshape(n_seq, n_q, D_MODEL),
            st("kp"), st("vp"), st("cbp"), st("cfp"),
            st("ks"), st("vs"), st("cbs"), st("cfs"), st("cvs"))
```

```python
import functools

import numpy as np
import jax
import jax.numpy as jnp
from jax import lax
from jax.experimental import pallas as pl
from jax.experimental.pallas import tpu as pltpu

F32 = jnp.float32
BF16 = jnp.bfloat16

D_MODEL = 2048
HEAD_DIM = 128
W_A = 1024
W_B = 512
W_C = 512
H_A = W_A // HEAD_DIM
H_C = W_C // HEAD_DIM
N_HEADS = D_MODEL // HEAD_DIM
IN_W = 3 * W_A + 3 * W_B + 2 * W_C
REST_W = IN_W - 3 * W_A
D_FF = 5504
CHUNK = 128
PAGE = 128
PLE_DIM = 256
EPS = 1e-6
SCALE = HEAD_DIM ** -0.5
LOG2E = float(np.log2(np.e))
Q_PRESCALE = SCALE * LOG2E

LANE = 128
SUBLANE = 8
FF_TILE = 512
D_FF_PAD = ((D_FF + FF_TILE - 1) // FF_TILE) * FF_TILE
IN_TILE = 512
VMEM_LIMIT = 56 * 1024 * 1024


def _cparams(*sem):
    return pltpu.CompilerParams(dimension_semantics=sem, vmem_limit_bytes=VMEM_LIMIT)


def _rms_rows(x, g):
    ms = jnp.mean(x * x, axis=-1, keepdims=True)
    return x * lax.rsqrt(ms + EPS) * g


def _gelu_tanh(x):
    c = np.float32(np.sqrt(2.0 / np.pi))
    return 0.5 * x * (1.0 + jnp.tanh(c * (x + 0.044715 * (x * x * x))))


def _softplus2(t):
    return jnp.maximum(t, 0.0) + jnp.log2(1.0 + jnp.exp2(-jnp.abs(t)))


def _split_bf16(x):
    hi = x.astype(BF16)
    lo = (x - hi.astype(F32)).astype(BF16)
    return hi, lo


def _in_proj_kernel(x_ref, g_ref, w_ref, qkv_ref, k_ref, v_ref, rest_ref, xn_ref):
    j = pl.program_id(1)

    @pl.when(j == 0)
    def _():
        xn_ref[...] = _rms_rows(x_ref[...], g_ref[...]).astype(BF16)

    r = jnp.dot(xn_ref[...], w_ref[...], preferred_element_type=F32)
    nq = W_A // IN_TILE

    @pl.when(j < nq)
    def _():
        qkv_ref[...] = (r * Q_PRESCALE).astype(BF16)

    @pl.when(jnp.logical_and(j >= nq, j < 3 * nq))
    def _():
        qkv_ref[...] = r.astype(BF16)

    @pl.when(jnp.logical_and(j >= nq, j < 2 * nq))
    def _():
        k_ref[...] = r

    @pl.when(jnp.logical_and(j >= 2 * nq, j < 3 * nq))
    def _():
        v_ref[...] = r

    @pl.when(j >= 3 * nq)
    def _():
        rest_ref[...] = r


def _in_proj(x, g, w, tm):
    m = x.shape[0]
    nq = W_A // IN_TILE
    nj = IN_W // IN_TILE
    nrest = REST_W // IN_TILE
    return pl.pallas_call(
        _in_proj_kernel,
        grid=(m // tm, nj),
        in_specs=[
            pl.BlockSpec((tm, D_MODEL), lambda i, j: (i, 0)),
            pl.BlockSpec((1, D_MODEL), lambda i, j: (0, 0)),
            pl.BlockSpec((D_MODEL, IN_TILE), lambda i, j: (0, j)),
        ],
        out_specs=[
            pl.BlockSpec((tm, IN_TILE), lambda i, j: (i, jnp.minimum(j, 3 * nq - 1))),
            pl.BlockSpec((tm, IN_TILE), lambda i, j: (i, jnp.clip(j - nq, 0, nq - 1))),
            pl.BlockSpec((tm, IN_TILE), lambda i, j: (i, jnp.clip(j - 2 * nq, 0, nq - 1))),
            pl.BlockSpec((tm, IN_TILE), lambda i, j: (i, jnp.clip(j - 3 * nq, 0, nrest - 1))),
        ],
        out_shape=[
            jax.ShapeDtypeStruct((m, 3 * W_A), BF16),
            jax.ShapeDtypeStruct((m, W_A), F32),
            jax.ShapeDtypeStruct((m, W_A), F32),
            jax.ShapeDtypeStruct((m, REST_W), F32),
        ],
        scratch_shapes=[pltpu.VMEM((tm, D_MODEL), BF16)],
        compiler_params=_cparams("arbitrary", "arbitrary"),
        name="in_proj",
    )(x, g, w)


ATT_TQ = 256
ATT_TK = 128
ATT_SPAN = 512
assert ATT_SPAN % ATT_TQ == 0


def _cumsum_matrix():
    j = np.arange(ATT_TK)[:, None]
    s = np.arange(ATT_TK)[None, :]
    u = (j > s).astype(np.float32)
    blk = np.concatenate([u, np.ones((ATT_TK, ATT_TK), np.float32)], axis=1)
    return jnp.asarray(np.concatenate([blk, blk], axis=0), dtype=BF16)


def _attn_kernel(b_ref, q_ref, k_ref, v_ref, u_ref, o_ref, acc_ref, carry_ref):
    h = pl.program_id(0)
    i = pl.program_id(1)
    bias2 = b_ref[h] * LOG2E
    q = q_ref[...]
    acc_ref[...] = jnp.zeros_like(acc_ref)
    carry_ref[...] = jnp.zeros_like(carry_ref)

    def scores(start, nt):
        start = pl.multiple_of(start, ATT_SPAN)
        k = k_ref[pl.ds(start, nt * ATT_TK), :]
        return lax.dot_general(q, k, (((1,), (1,)), ((), ())), preferred_element_type=F32) + bias2

    def span(t, start, nt, masked):
        n = nt * ATT_TK
        start = pl.multiple_of(start, ATT_SPAN)
        v = v_ref[pl.ds(start, n), :]
        sp = _softplus2(t)
        if masked:
            row = lax.broadcasted_iota(jnp.int32, (ATT_TQ, n), 0) + i * ATT_TQ
            col = lax.broadcasted_iota(jnp.int32, (ATT_TQ, n), 1) + start
            valid = col < row
            sp = jnp.where(valid, sp, 0.0)
        hi, lo = _split_bf16(sp)
        hl = jnp.concatenate(
            [jnp.concatenate([hi[:, j * ATT_TK:(j + 1) * ATT_TK], lo[:, j * ATT_TK:(j + 1) * ATT_TK]], axis=1)
             for j in range(nt)], axis=0)
        r = jnp.dot(hl, u_ref[...], preferred_element_type=F32)
        x = t - sp
        c = carry_ref[...]
        parts = [None] * nt
        for j in range(nt - 1, -1, -1):
            rj = r[j * ATT_TQ:(j + 1) * ATT_TQ]
            a = jnp.exp2(x[:, j * ATT_TK:(j + 1) * ATT_TK] - rj[:, :ATT_TK] - c)
            if masked:
                a = jnp.where(valid[:, j * ATT_TK:(j + 1) * ATT_TK], a, 0.0)
            parts[j] = a.astype(BF16)
            c = c + rj[:, ATT_TK:]
        carry_ref[...] = c
        acc_ref[...] += jnp.dot(jnp.concatenate(parts, axis=1), v, preferred_element_type=F32)

    nt_span = ATT_SPAN // ATT_TK
    n_full = (i * ATT_TQ) // ATT_SPAN
    span(scores(n_full * ATT_SPAN, nt_span), n_full * ATT_SPAN, nt_span, True)

    @pl.when(n_full % 2 == 1)
    def _():
        span(scores((n_full - 1) * ATT_SPAN, nt_span), (n_full - 1) * ATT_SPAN, nt_span, False)

    n_pairs = n_full // 2

    def pair_start(it):
        return jnp.maximum(n_pairs - 1 - it, 0) * (2 * ATT_SPAN)

    def body(it, t):
        t_next = scores(pair_start(it + 1), 2 * nt_span)
        span(t, pair_start(it), 2 * nt_span, False)
        return t_next

    lax.fori_loop(0, n_pairs, body, scores(pair_start(0), 2 * nt_span))
    o_ref[...] = acc_ref[...]


def _attention_prompt(qkv, b_sb):
    s = qkv.shape[0]
    return pl.pallas_call(
        _attn_kernel,
        grid=(H_A, s // ATT_TQ),
        in_specs=[
            pl.BlockSpec(memory_space=pltpu.SMEM),
            pl.BlockSpec((ATT_TQ, HEAD_DIM), lambda h, i: (i, h)),
            pl.BlockSpec((s, HEAD_DIM), lambda h, i: (0, H_A + h)),
            pl.BlockSpec((s, HEAD_DIM), lambda h, i: (0, 2 * H_A + h)),
            pl.BlockSpec((2 * ATT_TK, 2 * ATT_TK), lambda h, i: (0, 0)),
        ],
        out_specs=pl.BlockSpec((ATT_TQ, HEAD_DIM), lambda h, i: (i, h)),
        out_shape=jax.ShapeDtypeStruct((s, W_A), F32),
        scratch_shapes=[pltpu.VMEM((ATT_TQ, HEAD_DIM), F32), pltpu.VMEM((ATT_TQ, ATT_TK), F32)],
        compiler_params=_cparams("arbitrary", "arbitrary"),
        name="attn_prompt",
    )(b_sb, qkv, qkv, qkv, _cumsum_matrix())


SCOLS = LANE


SATT_PAGES = 4


def _sattn_kernel(pt_ref, qbd_ref, knew_ref, vnew_ref, *refs, n_q):
    del pt_ref
    kp_refs = refs[:SATT_PAGES]
    vp_refs = refs[SATT_PAGES:2 * SATT_PAGES]
    brow_ref, nmask_ref, l_ref, hm_ref, o_ref, acc_ref, carry_ref = refs[2 * SATT_PAGES:]
    p = pl.program_id(1)
    qbd = qbd_ref[...]
    brow = brow_ref[...]
    nrow = n_q * H_A

    def heads_to_lanes(ref):
        return jnp.concatenate(
            [ref[pl.ds(hh, PAGE, stride=H_A), :].astype(BF16) for hh in range(H_A)], axis=1)

    lmat = l_ref[...]

    def page_scores(kb, mask):
        zt = jnp.dot(kb, qbd, preferred_element_type=F32) + brow
        sp = _softplus2(zt)
        if mask is not None:
            sp = jnp.where(mask > 0.0, sp, 0.0)
        hi, lo = _split_bf16(sp)
        later = (jnp.dot(lmat, hi, preferred_element_type=F32)
                 + jnp.dot(lmat, lo, preferred_element_type=F32))
        return zt - sp - later, jnp.sum(sp, axis=0, keepdims=True)

    def page_out(x, c, vb, mask):
        a = jnp.exp2(x - c)
        if mask is not None:
            a = jnp.where(mask > 0.0, a, 0.0)
        at = a.T[:nrow].astype(BF16)
        return jnp.dot(at, vb, preferred_element_type=F32)

    @pl.when(p == 0)
    def _():
        mask = nmask_ref[...]
        x, tot = page_scores(knew_ref[...], mask)
        acc_ref[...] = page_out(x, jnp.zeros_like(tot), vnew_ref[...], mask)
        carry_ref[...] = tot

    xs, tots = zip(*[page_scores(heads_to_lanes(r), None) for r in kp_refs])
    c = carry_ref[...]
    out = None
    for g in range(SATT_PAGES - 1, -1, -1):
        o = page_out(xs[g], c, heads_to_lanes(vp_refs[g]), None)
        out = o if out is None else out + o
        c = c + tots[g]
    carry_ref[...] = c
    acc_ref[...] += out

    @pl.when(p == pl.num_programs(1) - 1)
    def _():
        res = acc_ref[...] * hm_ref[...]
        o_ref[...] = jnp.sum(res.reshape(n_q, H_A, W_A), axis=1)


def _attention_sample(qkv, k_new, v_new, b_sb, cache_k, cache_v, page_table, layer):
    n_seq, n_pages = page_table.shape
    n_q = qkv.shape[0] // n_seq
    nrow = n_q * H_A
    q = qkv[:, :W_A].reshape(n_seq, n_q, H_A, HEAD_DIM)
    eye = jnp.eye(H_A, dtype=BF16)
    qbd = jnp.einsum("bthd,hg->bhdtg", q, eye).reshape(n_seq, W_A, nrow)
    qbd = jnp.pad(qbd, ((0, 0), (0, 0), (0, SCOLS - nrow)))
    pad_rows = ((0, 0), (0, PAGE - n_q), (0, 0))
    knew = jnp.pad(k_new.reshape(n_seq, n_q, W_A), pad_rows).astype(BF16)
    vnew = jnp.pad(v_new.reshape(n_seq, n_q, W_A), pad_rows).astype(BF16)
    cols = np.arange(SCOLS)
    brow = jnp.where(cols < nrow, b_sb[cols % H_A] * LOG2E, 0.0).reshape(1, SCOLS).astype(F32)
    keys = np.arange(PAGE)[:, None]
    nmask = jnp.asarray(((cols[None, :] < nrow) & (keys < cols[None, :] // H_A)).astype(np.float32))
    lmat = jnp.asarray((np.arange(PAGE)[None, :] > keys).astype(np.float32), dtype=BF16)
    hm = jnp.asarray((np.arange(W_A)[None, :] // HEAD_DIM
                      == np.arange(nrow)[:, None] % H_A).astype(np.float32))
    n_pool = cache_k.shape[1]
    ck = cache_k.reshape(cache_k.shape[0], n_pool, PAGE * H_A, HEAD_DIM)
    cv = cache_v.reshape(cache_v.shape[0], n_pool, PAGE * H_A, HEAD_DIM)
    assert n_pages % SATT_PAGES == 0
    n_steps = n_pages // SATT_PAGES

    def page_spec(r):
        return pl.BlockSpec(
            (None, None, PAGE * H_A, HEAD_DIM),
            lambda b, p, pt: (layer, pt[b, n_pages - (p + 1) * SATT_PAGES + r], 0, 0))

    fixed = lambda b, p, pt: (0, 0)
    page_specs = [page_spec(r) for r in range(SATT_PAGES)]
    grid_spec = pltpu.PrefetchScalarGridSpec(
        num_scalar_prefetch=1,
        grid=(n_seq, n_steps),
        in_specs=[
            pl.BlockSpec((None, W_A, SCOLS), lambda b, p, pt: (b, 0, 0)),
            pl.BlockSpec((None, PAGE, W_A), lambda b, p, pt: (b, 0, 0)),
            pl.BlockSpec((None, PAGE, W_A), lambda b, p, pt: (b, 0, 0)),
            *page_specs, *page_specs,
            pl.BlockSpec((1, SCOLS), fixed),
            pl.BlockSpec((PAGE, SCOLS), fixed),
            pl.BlockSpec((PAGE, PAGE), fixed),
            pl.BlockSpec((nrow, W_A), fixed),
        ],
        out_specs=pl.BlockSpec((None, n_q, W_A), lambda b, p, pt: (b, 0, 0)),
        scratch_shapes=[pltpu.VMEM((nrow, W_A), F32), pltpu.VMEM((1, SCOLS), F32)],
    )
    out = pl.pallas_call(
        functools.partial(_sattn_kernel, n_q=n_q),
        grid_spec=grid_spec,
        out_shape=jax.ShapeDtypeStruct((n_seq, n_q, W_A), F32),
        compiler_params=_cparams("arbitrary", "arbitrary"),
        name="attn_sample",
    )(page_table, qbd, knew, vnew, *([ck] * SATT_PAGES), *([cv] * SATT_PAGES), brow, nmask, lmat, hm)
    return out.reshape(n_seq * n_q, W_A)


def _conv3_rows(x, cw_ref, buf_ref, prev8, p1, p2, seq_len):
    tm = x.shape[0]
    buf_ref[pl.ds(0, SUBLANE), :] = prev8
    buf_ref[pl.ds(SUBLANE, tm), :] = x
    x1 = buf_ref[pl.ds(SUBLANE - 1, tm), :]
    x2 = buf_ref[pl.ds(SUBLANE - 2, tm), :]
    if p1 is not None:
        t = lax.broadcasted_iota(jnp.int32, x.shape, 0) % seq_len
        x1 = jnp.where(t == 0, p1, x1)
        x2 = jnp.where(t < 2, p2, x2)
    cw = cw_ref[...]
    return cw[0:1, :] * x2 + cw[1:2, :] * x1 + cw[2:3, :] * x


def _mix_kernel(*refs, sample, seq_len):
    if sample:
        (h_ref, ya_ref, rest_ref, cw_ref, gv_ref, ws_ref, bs_ref, go_ref, wo_ref, p1_ref, p2_ref,
         out_ref, x_ref, vv_ref, buf_ref, y_ref) = refs
    else:
        (h_ref, ya_ref, rest_ref, cw_ref, gv_ref, ws_ref, bs_ref, go_ref, wo_ref,
         out_ref, x_ref, carry_ref, buf_ref, y_ref) = refs
    tm = h_ref.shape[0]
    i = pl.program_id(0)

    bg = rest_ref[:, 0:W_B]
    x = rest_ref[:, W_B:2 * W_B] * rest_ref[:, 2 * W_B:3 * W_B]
    x_ref[...] = x
    if sample:
        conv = _conv3_rows(x, cw_ref, buf_ref, jnp.zeros((SUBLANE, W_B), F32),
                           p1_ref[...], p2_ref[...], seq_len)
    else:
        @pl.when(i == 0)
        def _():
            carry_ref[...] = jnp.zeros_like(carry_ref)
        conv = _conv3_rows(x, cw_ref, buf_ref, carry_ref[...], None, None, seq_len)
        carry_ref[...] = x[tm - SUBLANE:, :]
    yb = bg * conv

    go = go_ref[...]

    def put_head(hidx, yh):
        ms = jnp.mean(yh * yh, axis=-1, keepdims=True)
        lo = hidx * HEAD_DIM
        y_ref[:, lo:lo + HEAD_DIM] = (yh * lax.rsqrt(ms + EPS) * go[:, lo:lo + HEAD_DIM]).astype(BF16)

    for hh in range(H_A):
        put_head(hh, ya_ref[:, hh * HEAD_DIM:(hh + 1) * HEAD_DIM])
    for hh in range(W_B // HEAD_DIM):
        put_head(H_A + hh, yb[:, hh * HEAD_DIM:(hh + 1) * HEAD_DIM])

    gv = gv_ref[...]
    bs = bs_ref[...]
    for hh in range(H_C):
        lo = hh * HEAD_DIM
        u = _gelu_tanh(rest_ref[:, 3 * W_B + lo:3 * W_B + lo + HEAD_DIM])
        gvc = _gelu_tanh(rest_ref[:, 3 * W_B + W_C + lo:3 * W_B + W_C + lo + HEAD_DIM])
        ms = jnp.mean(gvc * gvc, axis=-1, keepdims=True)
        vv = gvc * lax.rsqrt(ms + EPS) * gv[:, lo:lo + HEAD_DIM]
        if sample:
            vv_ref[:, lo:lo + HEAD_DIM] = vv
        vvb = vv.astype(BF16)
        w = ws_ref[hh]
        parts = []
        for c in range(tm // CHUNK):
            parts.append(jnp.dot(w, vvb[c * CHUNK:(c + 1) * CHUNK, :], preferred_element_type=F32)
                         + bs[:, lo:lo + HEAD_DIM])
        s = parts[0] if len(parts) == 1 else jnp.concatenate(parts, axis=0)
        put_head(H_A + W_B // HEAD_DIM + hh, u * s)

    out_ref[...] = h_ref[...] + jnp.dot(y_ref[...], wo_ref[...], preferred_element_type=F32)


def _mix(h, ya, rest, cw, g_v, ws, bs, g_out, w_o, tm, p1=None, p2=None, seq_len=1):
    m = h.shape[0]
    sample = p1 is not None
    row = lambda i: (i, 0)
    fixed = lambda i: (0, 0)
    in_specs = [
        pl.BlockSpec((tm, D_MODEL), row),
        pl.BlockSpec((tm, W_A), row),
        pl.BlockSpec((tm, REST_W), row),
        pl.BlockSpec((3, W_B), fixed),
        pl.BlockSpec((1, W_C), fixed),
        pl.BlockSpec((H_C, CHUNK, CHUNK), lambda i: (0, 0, 0)),
        pl.BlockSpec((CHUNK, W_C), fixed),
        pl.BlockSpec((1, D_MODEL), fixed),
        pl.BlockSpec((D_MODEL, D_MODEL), fixed),
    ]
    args = [h, ya, rest, cw, g_v, ws, bs, g_out, w_o]
    out_specs = [pl.BlockSpec((tm, D_MODEL), row), pl.BlockSpec((tm, W_B), row)]
    out_shape = [jax.ShapeDtypeStruct((m, D_MODEL), F32), jax.ShapeDtypeStruct((m, W_B), F32)]
    scratch = []
    if sample:
        in_specs += [pl.BlockSpec((tm, W_B), row), pl.BlockSpec((tm, W_B), row)]
        args += [p1, p2]
        out_specs.append(pl.BlockSpec((tm, W_C), row))
        out_shape.append(jax.ShapeDtypeStruct((m, W_C), F32))
    else:
        scratch.append(pltpu.VMEM((SUBLANE, W_B), F32))
    scratch += [pltpu.VMEM((tm + SUBLANE, W_B), F32), pltpu.VMEM((tm, D_MODEL), BF16)]
    return pl.pallas_call(
        functools.partial(_mix_kernel, sample=sample, seq_len=seq_len),
        grid=(m // tm,),
        in_specs=in_specs,
        out_specs=out_specs,
        out_shape=out_shape,
        scratch_shapes=scratch,
        compiler_params=_cparams("arbitrary"),
        name="mix_sample" if sample else "mix_prompt",
    )(*args)


def _ffn_kernel(*refs, sample, seq_len):
    if sample:
        (h_ref, g_ref, wg_ref, wv_ref, cwg_ref, cwv_ref, wd_ref, p1g_ref, p2g_ref, p1v_ref, p2v_ref,
         out_ref, upg_ref, upv_ref, xn_ref, acc_ref, buf_ref) = refs
    else:
        (h_ref, g_ref, wg_ref, wv_ref, cwg_ref, cwv_ref, wd_ref,
         out_ref, upg_ref, upv_ref, xn_ref, acc_ref, buf_ref, carry_ref) = refs
    tm = h_ref.shape[0]
    i = pl.program_id(0)
    j = pl.program_id(1)

    @pl.when(j == 0)
    def _():
        xn_ref[...] = _rms_rows(h_ref[...], g_ref[...]).astype(BF16)
        acc_ref[...] = jnp.zeros_like(acc_ref)

    if not sample:
        @pl.when(jnp.logical_and(i == 0, j == 0))
        def _():
            carry_ref[...] = jnp.zeros_like(carry_ref)

    def branch(w_ref, cw_ref, idx, up_ref, p1_ref, p2_ref):
        up = jnp.dot(xn_ref[...], w_ref[...], preferred_element_type=F32)
        if sample:
            up_ref[...] = up
            return _conv3_rows(up, cw_ref, buf_ref, jnp.zeros((SUBLANE, FF_TILE), F32),
                               p1_ref[...], p2_ref[...], seq_len)
        conv = _conv3_rows(up, cw_ref, buf_ref, carry_ref[idx, j], None, None, seq_len)
        last = up[tm - SUBLANE:, :]
        carry_ref[idx, j] = last
        up_ref[...] = last
        return conv

    if sample:
        gc = branch(wg_ref, cwg_ref, 0, upg_ref, p1g_ref, p2g_ref)
        vc = branch(wv_ref, cwv_ref, 1, upv_ref, p1v_ref, p2v_ref)
    else:
        gc = branch(wg_ref, cwg_ref, 0, upg_ref, None, None)
        vc = branch(wv_ref, cwv_ref, 1, upv_ref, None, None)
    act = gc * (1.0 / (1.0 + jnp.exp(-gc))) * vc
    acc_ref[...] += jnp.dot(act.astype(BF16), wd_ref[...], preferred_element_type=F32)

    @pl.when(j == pl.num_programs(1) - 1)
    def _():
        out_ref[...] = h_ref[...] + acc_ref[...]


def _ffn(h, g, wg, wv, cwg, cwv, wd, tm, prevs=None, seq_len=1):
    m = h.shape[0]
    sample = prevs is not None
    nj = D_FF_PAD // FF_TILE
    row = lambda i, j: (i, 0)
    col = lambda i, j: (0, j)
    in_specs = [
        pl.BlockSpec((tm, D_MODEL), row),
        pl.BlockSpec((1, D_MODEL), lambda i, j: (0, 0)),
        pl.BlockSpec((D_MODEL, FF_TILE), col),
        pl.BlockSpec((D_MODEL, FF_TILE), col),
        pl.BlockSpec((3, FF_TILE), col),
        pl.BlockSpec((3, FF_TILE), col),
        pl.BlockSpec((FF_TILE, D_MODEL), lambda i, j: (j, 0)),
    ]
    args = [h, g, wg, wv, cwg, cwv, wd]
    up_rows = tm if sample else SUBLANE
    up_spec = pl.BlockSpec((up_rows, FF_TILE), lambda i, j: (i, j))
    up_shape = jax.ShapeDtypeStruct((m // tm * up_rows, D_FF_PAD), F32)
    out_specs = [pl.BlockSpec((tm, D_MODEL), row), up_spec, up_spec]
    out_shape = [jax.ShapeDtypeStruct((m, D_MODEL), F32), up_shape, up_shape]
    scratch = [pltpu.VMEM((tm, D_MODEL), BF16), pltpu.VMEM((tm, D_MODEL), F32),
               pltpu.VMEM((tm + SUBLANE, FF_TILE), F32)]
    if sample:
        in_specs += [pl.BlockSpec((tm, FF_TILE), col)] * 4
        args += list(prevs)
    else:
        scratch.append(pltpu.VMEM((2, nj, SUBLANE, FF_TILE), F32))
    return pl.pallas_call(
        functools.partial(_ffn_kernel, sample=sample, seq_len=seq_len),
        grid=(m // tm, nj),
        in_specs=in_specs,
        out_specs=out_specs,
        out_shape=out_shape,
        scratch_shapes=scratch,
        compiler_params=_cparams("arbitrary", "arbitrary"),
        name="ffn_sample" if sample else "ffn_prompt",
    )(*args)


def _ple_kernel(h_ref, g_ref, p_ref, wpg_ref, wpp_ref, gf_ref, out_ref, *, final):
    h = h_ref[...]
    xn = _rms_rows(h, g_ref[...]).astype(BF16)
    gate = jnp.dot(xn, wpg_ref[...], preferred_element_type=F32)
    gate = 1.0 / (1.0 + jnp.exp(-gate))
    proj = jnp.dot(p_ref[...].astype(BF16), wpp_ref[...], preferred_element_type=F32)
    hn = h + gate * proj
    if final:
        hn = _rms_rows(hn, gf_ref[...])
    out_ref[...] = hn


def _ple(h, g, p, wpg, wpp, g_final, tm, final):
    m = h.shape[0]
    row = lambda i: (i, 0)
    fixed = lambda i: (0, 0)
    return pl.pallas_call(
        functools.partial(_ple_kernel, final=final),
        grid=(m // tm,),
        in_specs=[
            pl.BlockSpec((tm, D_MODEL), row),
            pl.BlockSpec((1, D_MODEL), fixed),
            pl.BlockSpec((tm, PLE_DIM), row),
            pl.BlockSpec((D_MODEL, D_MODEL), fixed),
            pl.BlockSpec((PLE_DIM, D_MODEL), fixed),
            pl.BlockSpec((1, D_MODEL), fixed),
        ],
        out_specs=pl.BlockSpec((tm, D_MODEL), row),
        out_shape=jax.ShapeDtypeStruct((m, D_MODEL), F32),
        compiler_params=_cparams("arbitrary"),
        name="ple",
    )(h, g, p, wpg, wpp, g_final)


def _expand_state(state, seq_len):
    b, _, c = state.shape
    zeros = jnp.zeros((b, seq_len - 1, c), state.dtype)
    p1 = jnp.concatenate([state[:, 1:2], zeros], axis=1).reshape(b * seq_len, c)
    p2 = jnp.concatenate([state[:, 0:1], state[:, 1:2], zeros[:, 1:]], axis=1).reshape(b * seq_len, c)
    return p1, p2


def _pad_ff(x):
    return jnp.pad(x, ((0, 0), (0, D_FF_PAD - D_FF)))


def kernel(x_prompt, x_sample, cache_k, cache_v, state_conv_mix, state_conv_ffn, page_table,
           p_prompt, p_sample, g_mix, w_in, b_sb, conv_mix_w, g_v, w_s, b_s, g_out, w_o,
           g_ffn, w_up, conv_ffn_w, w_down, g_ple, w_ple_gate, w_ple_proj, g_final):
    depth = w_in.shape[0]
    bp, seq, _ = x_prompt.shape
    n_seq, n_q, _ = x_sample.shape
    assert bp == 1 and n_q >= 2 and n_q * n_seq == CHUNK
    ms = n_seq * n_q
    hp = x_prompt.reshape(seq, D_MODEL)
    hs = x_sample.reshape(ms, D_MODEL)
    gfin = g_final.reshape(1, D_MODEL)
    tril = jnp.tril(jnp.ones((CHUNK, CHUNK), F32))

    outs = {k: [] for k in ("kp", "vp", "cbp", "cfp", "ks", "vs", "cbs", "cfs", "cvs")}
    for l in range(depth):
        w_in_b = w_in[l].astype(BF16)
        w_o_b = w_o[l].astype(BF16)
        wg_b = _pad_ff(w_up[l][:, :D_FF]).astype(BF16)
        wv_b = _pad_ff(w_up[l][:, D_FF:]).astype(BF16)
        cwg = _pad_ff(conv_ffn_w[l][:, :D_FF])
        cwv = _pad_ff(conv_ffn_w[l][:, D_FF:])
        wd_b = jnp.pad(w_down[l], ((0, D_FF_PAD - D_FF), (0, 0))).astype(BF16)
        wpg_b = w_ple_gate[l].astype(BF16)
        wpp_b = w_ple_proj[l].astype(BF16)
        gm = g_mix[l].reshape(1, D_MODEL)
        gv = g_v[l].reshape(1, W_C)
        go = g_out[l].reshape(1, D_MODEL)
        gf = g_ffn[l].reshape(1, D_MODEL)
        gp = g_ple[l].reshape(1, D_MODEL)
        final = l == depth - 1

        ws_p = (w_s[l] * tril).astype(BF16)
        bs_p = jnp.repeat(b_s[l].T, HEAD_DIM, axis=1)
        wm_q = (w_s[l] * tril)[:, :n_q, :n_q]
        ws_s = jnp.einsum("ab,hts->hatbs", jnp.eye(n_seq, dtype=F32), wm_q).reshape(H_C, ms, ms).astype(BF16)
        bs_s = jnp.repeat(jnp.tile(b_s[l][:, :n_q].T, (n_seq, 1)), HEAD_DIM, axis=1)

        qkv, kp, vp, rest = _in_proj(hp, gm, w_in_b, 1024)
        ya = _attention_prompt(qkv, b_sb[l])
        hp, xcp = _mix(hp, ya, rest, conv_mix_w[l], gv, ws_p, bs_p, go, w_o_b, 256)
        hp, cfg, cfv = _ffn(hp, gf, wg_b, wv_b, cwg, cwv, wd_b, 512)
        hp = _ple(hp, gp, p_prompt[l].reshape(seq, PLE_DIM), wpg_b, wpp_b, gfin, 512, final)
        outs["kp"].append(kp.reshape(1, seq, H_A, HEAD_DIM))
        outs["vp"].append(vp.reshape(1, seq, H_A, HEAD_DIM))
        outs["cbp"].append(xcp[seq - 2:].reshape(1, 2, W_B))
        outs["cfp"].append(jnp.concatenate([cfg[-2:, :D_FF], cfv[-2:, :D_FF]],
                                           axis=-1).reshape(1, 2, 2 * D_FF))

        qkv_s, ks, vs, rest_s = _in_proj(hs, gm, w_in_b, ms)
        ya_s = _attention_sample(qkv_s, ks, vs, b_sb[l], cache_k, cache_v, page_table, l)
        p1m, p2m = _expand_state(state_conv_mix[l], n_q)
        hs, xcs, vvs = _mix(hs, ya_s, rest_s, conv_mix_w[l], gv, ws_s, bs_s, go, w_o_b, ms,
                            p1=p1m, p2=p2m, seq_len=n_q)
        p1g, p2g = _expand_state(_pad_ff(state_conv_ffn[l][..., :D_FF].reshape(n_seq * 2, D_FF))
                                 .reshape(n_seq, 2, D_FF_PAD), n_q)
        p1v, p2v = _expand_state(_pad_ff(state_conv_ffn[l][..., D_FF:].reshape(n_seq * 2, D_FF))
                                 .reshape(n_seq, 2, D_FF_PAD), n_q)
        hs, upg, upv = _ffn(hs, gf, wg_b, wv_b, cwg, cwv, wd_b, ms,
                            prevs=(p1g, p2g, p1v, p2v), seq_len=n_q)
        hs = _ple(hs, gp, p_sample[l].reshape(ms, PLE_DIM), wpg_b, wpp_b, gfin, ms, final)
        outs["ks"].append(ks.reshape(n_seq, n_q, H_A, HEAD_DIM))
        outs["vs"].append(vs.reshape(n_seq, n_q, H_A, HEAD_DIM))
        outs["cbs"].append(xcs.reshape(n_seq, n_q, W_B)[:, n_q - 2:])
        up_s = jnp.concatenate([upg[:, :D_FF], upv[:, :D_FF]], axis=-1)
        outs["cfs"].append(up_s.reshape(n_seq, n_q, 2 * D_FF)[:, n_q - 2:])
        outs["cvs"].append(vvs.reshape(n_seq, n_q, W_C))

    st = lambda k: jnp.stack(outs[k])
    return (hp.reshape(1, seq, D_MODEL), hs.reshape(n_seq, n_q, D_MODEL),
            st("kp"), st("vp"), st("cbp"), st("cfp"),
            st("ks"), st("vs"), st("cbs"), st("cfs"), st("cvs"))
```

```python
import functools

import numpy as np
import jax
import jax.numpy as jnp
from jax import lax
from jax.experimental import pallas as pl
from jax.experimental.pallas import tpu as pltpu

F32 = jnp.float32
BF16 = jnp.bfloat16

D_MODEL = 2048
HEAD_DIM = 128
W_A = 1024
W_B = 512
W_C = 512
H_A = W_A // HEAD_DIM
H_C = W_C // HEAD_DIM
N_HEADS = D_MODEL // HEAD_DIM
IN_W = 3 * W_A + 3 * W_B + 2 * W_C
REST_W = IN_W - 3 * W_A
D_FF = 5504
CHUNK = 128
PAGE = 128
PLE_DIM = 256
EPS = 1e-6
SCALE = HEAD_DIM ** -0.5
LOG2E = float(np.log2(np.e))
Q_PRESCALE = SCALE * LOG2E

LANE = 128
SUBLANE = 8
FF_TILE = 512
D_FF_PAD = ((D_FF + FF_TILE - 1) // FF_TILE) * FF_TILE
IN_TILE = 512
VMEM_LIMIT = 56 * 1024 * 1024


def _cparams(*sem):
    return pltpu.CompilerParams(dimension_semantics=sem, vmem_limit_bytes=VMEM_LIMIT)


def _rms_rows(x, g):
    ms = jnp.mean(x * x, axis=-1, keepdims=True)
    return x * lax.rsqrt(ms + EPS) * g


def _gelu_tanh(x):
    c = np.float32(np.sqrt(2.0 / np.pi))
    return 0.5 * x * (1.0 + jnp.tanh(c * (x + 0.044715 * (x * x * x))))


def _softplus2(t):
    return jnp.maximum(t, 0.0) + jnp.log2(1.0 + jnp.exp2(-jnp.abs(t)))


def _split_bf16(x):
    hi = x.astype(BF16)
    lo = (x - hi.astype(F32)).astype(BF16)
    return hi, lo


def _in_proj_kernel(x_ref, g_ref, w_ref, qkv_ref, k_ref, v_ref, rest_ref, xn_ref, *, head_rows):
    j = pl.program_id(1)
    tm = x_ref.shape[0]

    @pl.when(j == 0)
    def _():
        xn_ref[...] = _rms_rows(x_ref[...], g_ref[...]).astype(BF16)

    r = jnp.dot(xn_ref[...], w_ref[...], preferred_element_type=F32)
    nq = W_A // IN_TILE
    heads_per_tile = IN_TILE // HEAD_DIM

    @pl.when(j < nq)
    def _():
        qkv_ref[...] = (r * Q_PRESCALE).astype(BF16)

    @pl.when(jnp.logical_and(j >= nq, j < 3 * nq))
    def _():
        qkv_ref[...] = r.astype(BF16)

    def put_f32(out_ref, first_tile):
        if not head_rows:
            @pl.when(jnp.logical_and(j >= first_tile, j < first_tile + nq))
            def _():
                out_ref[...] = r
            return
        for jj in range(nq):
            @pl.when(j == first_tile + jj)
            def _(jj=jj):
                for hh in range(heads_per_tile):
                    out_ref[pl.ds(jj * heads_per_tile + hh, tm, stride=H_A), :] = (
                        r[:, hh * HEAD_DIM:(hh + 1) * HEAD_DIM])

    put_f32(k_ref, nq)
    put_f32(v_ref, 2 * nq)

    @pl.when(j >= 3 * nq)
    def _():
        rest_ref[...] = r


def _in_proj(x, g, w, tm, head_rows):
    m = x.shape[0]
    nq = W_A // IN_TILE
    nj = IN_W // IN_TILE
    nrest = REST_W // IN_TILE
    if head_rows:
        kv_spec = lambda first: pl.BlockSpec((tm * H_A, HEAD_DIM), lambda i, j: (i, 0))
        kv_shape = jax.ShapeDtypeStruct((m * H_A, HEAD_DIM), F32)
    else:
        kv_spec = lambda first: pl.BlockSpec((tm, IN_TILE), lambda i, j: (i, jnp.clip(j - first, 0, nq - 1)))
        kv_shape = jax.ShapeDtypeStruct((m, W_A), F32)
    return pl.pallas_call(
        functools.partial(_in_proj_kernel, head_rows=head_rows),
        grid=(m // tm, nj),
        in_specs=[
            pl.BlockSpec((tm, D_MODEL), lambda i, j: (i, 0)),
            pl.BlockSpec((1, D_MODEL), lambda i, j: (0, 0)),
            pl.BlockSpec((D_MODEL, IN_TILE), lambda i, j: (0, j)),
        ],
        out_specs=[
            pl.BlockSpec((tm, IN_TILE), lambda i, j: (i, jnp.minimum(j, 3 * nq - 1))),
            kv_spec(nq),
            kv_spec(2 * nq),
            pl.BlockSpec((tm, IN_TILE), lambda i, j: (i, jnp.clip(j - 3 * nq, 0, nrest - 1))),
        ],
        out_shape=[
            jax.ShapeDtypeStruct((m, 3 * W_A), BF16),
            kv_shape,
            kv_shape,
            jax.ShapeDtypeStruct((m, REST_W), F32),
        ],
        scratch_shapes=[pltpu.VMEM((tm, D_MODEL), BF16)],
        compiler_params=_cparams("arbitrary", "arbitrary"),
        name="in_proj",
    )(x, g, w)


ATT_TQ = 256
ATT_TK = 128
ATT_SPAN = 512
ATT_HEADS = 4
ATT_QK_SKEW = 2
ATT_PREFETCH = False
ATT_SKEW = 2
assert ATT_SPAN % ATT_TQ == 0 and H_A % ATT_HEADS == 0


def _cumsum_matrix():
    j = np.arange(ATT_TK)[:, None]
    s = np.arange(ATT_TK)[None, :]
    u = (j > s).astype(np.float32)
    blk = np.concatenate([u, np.ones((ATT_TK, ATT_TK), np.float32)], axis=1)
    return jnp.asarray(np.concatenate([blk, blk], axis=0), dtype=BF16)


def _attn_kernel(b_ref, q_ref, k_ref, v_ref, u_ref, o_ref, acc_ref, carry_ref):
    i = pl.program_id(1)
    acc_ref[...] = jnp.zeros_like(acc_ref)
    carry_ref[...] = jnp.zeros_like(carry_ref)
    heads = range(ATT_HEADS)
    lanes = [slice(e * HEAD_DIM, (e + 1) * HEAD_DIM) for e in heads]
    bias2 = [b_ref[pl.program_id(0) * ATT_HEADS + e] * LOG2E for e in heads]

    def scores(start, nt):
        start = pl.multiple_of(start, ATT_SPAN)
        return tuple(
            lax.dot_general(q_ref[:, lanes[e]], k_ref[pl.ds(start, nt * ATT_TK), lanes[e]],
                            (((1,), (1,)), ((), ())), preferred_element_type=F32) + bias2[e]
            for e in heads)

    def span(ts, start, nt, masked):
        start = pl.multiple_of(start, ATT_SPAN)
        unit_w = 2 * ATT_TK
        units = [(e, u) for u in range(nt // 2 - 1, -1, -1) for e in heads]
        cs = [carry_ref[e] for e in heads]
        accs = [None for _ in heads]
        pending = {}

        def stage_scores(e, u):
            cols = slice(u * unit_w, (u + 1) * unit_w)
            t = logits.pop((e, u)) if ts is None else ts[e][:, cols]
            sp = _softplus2(t)
            valid = None
            if masked:
                row = lax.broadcasted_iota(jnp.int32, (ATT_TQ, unit_w), 0) + i * ATT_TQ
                col = lax.broadcasted_iota(jnp.int32, (ATT_TQ, unit_w), 1) + (start + u * unit_w)
                valid = col < row
                sp = jnp.where(valid, sp, 0.0)
            hi, lo = _split_bf16(sp)
            hl = jnp.concatenate(
                [jnp.concatenate([hi[:, j * ATT_TK:(j + 1) * ATT_TK], lo[:, j * ATT_TK:(j + 1) * ATT_TK]], axis=1)
                 for j in range(2)], axis=0)
            r = jnp.dot(hl, u_ref[...], preferred_element_type=F32)
            pending[(e, u)] = (t - sp, r, valid)

        def stage_out(e, u):
            x, r, valid = pending.pop((e, u))
            c = cs[e]
            parts = [None, None]
            for j in (1, 0):
                rj = r[j * ATT_TQ:(j + 1) * ATT_TQ]
                a = jnp.exp2(x[:, j * ATT_TK:(j + 1) * ATT_TK] - rj[:, :ATT_TK] - c)
                if masked:
                    a = jnp.where(valid[:, j * ATT_TK:(j + 1) * ATT_TK], a, 0.0)
                parts[j] = a.astype(BF16)
                c = c + rj[:, ATT_TK:]
            cs[e] = c
            v = v_ref[pl.ds(start + u * unit_w, unit_w), lanes[e]]
            o = jnp.dot(jnp.concatenate(parts, axis=1), v, preferred_element_type=F32)
            accs[e] = o if accs[e] is None else accs[e] + o

        logits = {}

        def stage_qk(e, u):
            k = k_ref[pl.ds(start + u * unit_w, unit_w), lanes[e]]
            logits[(e, u)] = lax.dot_general(q_ref[:, lanes[e]], k, (((1,), (1,)), ((), ())),
                                             preferred_element_type=F32) + bias2[e]

        n_units = len(units)
        qk_skew = ATT_QK_SKEW if ts is None else 0
        for idx in range(n_units + qk_skew + ATT_SKEW):
            if ts is None and idx < n_units:
                stage_qk(*units[idx])
            if 0 <= idx - qk_skew < n_units:
                stage_scores(*units[idx - qk_skew])
            if 0 <= idx - qk_skew - ATT_SKEW < n_units:
                stage_out(*units[idx - qk_skew - ATT_SKEW])
        for e in heads:
            carry_ref[e] = cs[e]
            acc_ref[e] += accs[e]


    nt_span = ATT_SPAN // ATT_TK
    n_full = (i * ATT_TQ) // ATT_SPAN
    span(None, n_full * ATT_SPAN, nt_span, True)

    @pl.when(n_full % 2 == 1)
    def _():
        span(None, (n_full - 1) * ATT_SPAN, nt_span, False)

    n_pairs = n_full // 2

    def pair_start(it):
        return jnp.maximum(n_pairs - 1 - it, 0) * (2 * ATT_SPAN)

    if ATT_PREFETCH:
        def body(it, t):
            t_next = scores(pair_start(it + 1), 2 * nt_span)
            span(t, pair_start(it), 2 * nt_span, False)
            return t_next

        lax.fori_loop(0, n_pairs, body, scores(pair_start(0), 2 * nt_span))
    else:
        def body(it, c):
            span(None, pair_start(it), 2 * nt_span, False)
            return c

        lax.fori_loop(0, n_pairs, body, 0)
    for e in heads:
        o_ref[:, lanes[e]] = acc_ref[e]


def _attention_prompt(qkv, b_sb):
    s = qkv.shape[0]
    width = ATT_HEADS * HEAD_DIM
    groups = H_A // ATT_HEADS
    return pl.pallas_call(
        _attn_kernel,
        grid=(groups, s // ATT_TQ),
        in_specs=[
            pl.BlockSpec(memory_space=pltpu.SMEM),
            pl.BlockSpec((ATT_TQ, width), lambda h, i: (i, h)),
            pl.BlockSpec((s, width), lambda h, i: (0, groups + h)),
            pl.BlockSpec((s, width), lambda h, i: (0, 2 * groups + h)),
            pl.BlockSpec((2 * ATT_TK, 2 * ATT_TK), lambda h, i: (0, 0)),
        ],
        out_specs=pl.BlockSpec((ATT_TQ, width), lambda h, i: (i, h)),
        out_shape=jax.ShapeDtypeStruct((s, W_A), F32),
        scratch_shapes=[pltpu.VMEM((ATT_HEADS, ATT_TQ, HEAD_DIM), F32),
                        pltpu.VMEM((ATT_HEADS, ATT_TQ, ATT_TK), F32)],
        compiler_params=_cparams("arbitrary", "arbitrary"),
        name="attn_prompt",
    )(b_sb, qkv, qkv, qkv, _cumsum_matrix())


SCOLS = LANE


SATT_PAGES = 8


def _sattn_kernel(pt_ref, qbd_ref, knew_ref, vnew_ref, *refs, n_q):
    del pt_ref
    kp_refs = refs[:SATT_PAGES]
    vp_refs = refs[SATT_PAGES:2 * SATT_PAGES]
    brow_ref, nmask_ref, l_ref, hm_ref, o_ref, acc_ref, carry_ref, new_ref = refs[2 * SATT_PAGES:]
    p = pl.program_id(1)
    qbd = qbd_ref[...]
    brow = brow_ref[...]
    nrow = n_q * H_A

    def heads_to_lanes(ref):
        return jnp.concatenate(
            [ref[pl.ds(hh, PAGE, stride=H_A), :].astype(BF16) for hh in range(H_A)], axis=1)

    lmat = l_ref[...]

    def page_scores(kb, mask):
        zt = jnp.dot(kb, qbd, preferred_element_type=F32) + brow
        sp = _softplus2(zt)
        if mask is not None:
            sp = jnp.where(mask > 0.0, sp, 0.0)
        hi, lo = _split_bf16(sp)
        later = (jnp.dot(lmat, hi, preferred_element_type=F32)
                 + jnp.dot(lmat, lo, preferred_element_type=F32))
        return zt - sp - later, jnp.sum(sp, axis=0, keepdims=True)

    def page_out(x, c, vb, mask):
        a = jnp.exp2(x - c)
        if mask is not None:
            a = jnp.where(mask > 0.0, a, 0.0)
        at = a.T[:nrow].astype(BF16)
        return jnp.dot(at, vb, preferred_element_type=F32)

    @pl.when(p == 0)
    def _():
        mask = nmask_ref[...]
        new_ref[...] = jnp.zeros_like(new_ref)
        new_ref[0, 0:n_q, :] = knew_ref[...]
        new_ref[1, 0:n_q, :] = vnew_ref[...]
        x, tot = page_scores(new_ref[0].astype(BF16), mask)
        acc_ref[...] = page_out(x, jnp.zeros_like(tot), new_ref[1].astype(BF16), mask)
        carry_ref[...] = tot

    pages = range(SATT_PAGES)
    zts = [jnp.dot(heads_to_lanes(r), qbd, preferred_element_type=F32) + brow for r in kp_refs]
    sps = [_softplus2(zt) for zt in zts]
    splits = [_split_bf16(sp) for sp in sps]
    laters = [jnp.dot(lmat, hi, preferred_element_type=F32) + jnp.dot(lmat, lo, preferred_element_type=F32)
              for hi, lo in splits]
    tots = [jnp.sum(sp, axis=0, keepdims=True) for sp in sps]
    cs = [None] * SATT_PAGES
    c = carry_ref[...]
    for g in range(SATT_PAGES - 1, -1, -1):
        cs[g] = c
        c = c + tots[g]
    carry_ref[...] = c
    ats = [jnp.exp2(zts[g] - sps[g] - laters[g] - cs[g]).T[:nrow].astype(BF16) for g in pages]
    outs = [jnp.dot(ats[g], heads_to_lanes(vp_refs[g]), preferred_element_type=F32) for g in pages]
    acc_ref[...] += sum(outs[1:], outs[0])

    @pl.when(p == pl.num_programs(1) - 1)
    def _():
        res = acc_ref[...] * hm_ref[...]
        o_ref[...] = jnp.sum(res.reshape(n_q, H_A, W_A), axis=1)


def _attention_sample(qkv, k_new, v_new, b_sb, cache_k, cache_v, page_table, layer):
    n_seq, n_pages = page_table.shape
    n_q = qkv.shape[0] // n_seq
    nrow = n_q * H_A
    q = qkv[:, :W_A].reshape(n_seq, n_q, H_A, HEAD_DIM)
    eye = jnp.eye(H_A, dtype=BF16)
    qbd = jnp.einsum("bthd,hg->bhdtg", q, eye).reshape(n_seq, W_A, nrow)
    qbd = jnp.pad(qbd, ((0, 0), (0, 0), (0, SCOLS - nrow)))
    knew = k_new.reshape(n_seq, n_q, W_A)
    vnew = v_new.reshape(n_seq, n_q, W_A)
    cols = np.arange(SCOLS)
    brow = jnp.where(cols < nrow, b_sb[cols % H_A] * LOG2E, 0.0).reshape(1, SCOLS).astype(F32)
    keys = np.arange(PAGE)[:, None]
    nmask = jnp.asarray(((cols[None, :] < nrow) & (keys < cols[None, :] // H_A)).astype(np.float32))
    lmat = jnp.asarray((np.arange(PAGE)[None, :] > keys).astype(np.float32), dtype=BF16)
    hm = jnp.asarray((np.arange(W_A)[None, :] // HEAD_DIM
                      == np.arange(nrow)[:, None] % H_A).astype(np.float32))
    n_pool = cache_k.shape[1]
    ck = cache_k.reshape(cache_k.shape[0], n_pool, PAGE * H_A, HEAD_DIM)
    cv = cache_v.reshape(cache_v.shape[0], n_pool, PAGE * H_A, HEAD_DIM)
    assert n_pages % SATT_PAGES == 0
    n_steps = n_pages // SATT_PAGES

    def page_spec(r):
        return pl.BlockSpec(
            (None, None, PAGE * H_A, HEAD_DIM),
            lambda b, p, pt: (layer, pt[b, n_pages - (p + 1) * SATT_PAGES + r], 0, 0))

    fixed = lambda b, p, pt: (0, 0)
    page_specs = [page_spec(r) for r in range(SATT_PAGES)]
    grid_spec = pltpu.PrefetchScalarGridSpec(
        num_scalar_prefetch=1,
        grid=(n_seq, n_steps),
        in_specs=[
            pl.BlockSpec((None, W_A, SCOLS), lambda b, p, pt: (b, 0, 0)),
            pl.BlockSpec((None, n_q, W_A), lambda b, p, pt: (b, 0, 0)),
            pl.BlockSpec((None, n_q, W_A), lambda b, p, pt: (b, 0, 0)),
            *page_specs, *page_specs,
            pl.BlockSpec((1, SCOLS), fixed),
            pl.BlockSpec((PAGE, SCOLS), fixed),
            pl.BlockSpec((PAGE, PAGE), fixed),
            pl.BlockSpec((nrow, W_A), fixed),
        ],
        out_specs=pl.BlockSpec((None, n_q, W_A), lambda b, p, pt: (b, 0, 0)),
        scratch_shapes=[pltpu.VMEM((nrow, W_A), F32), pltpu.VMEM((1, SCOLS), F32),
                        pltpu.VMEM((2, PAGE, W_A), F32)],
    )
    out = pl.pallas_call(
        functools.partial(_sattn_kernel, n_q=n_q),
        grid_spec=grid_spec,
        out_shape=jax.ShapeDtypeStruct((n_seq, n_q, W_A), F32),
        compiler_params=_cparams("arbitrary", "arbitrary"),
        name="attn_sample",
    )(page_table, qbd, knew, vnew, *([ck] * SATT_PAGES), *([cv] * SATT_PAGES), brow, nmask, lmat, hm)
    return out.reshape(n_seq * n_q, W_A)


def _conv3_rows(x, cw_ref, buf_ref, prev8, p1, p2, seq_len):
    tm = x.shape[0]
    buf_ref[pl.ds(0, SUBLANE), :] = prev8
    buf_ref[pl.ds(SUBLANE, tm), :] = x
    x1 = buf_ref[pl.ds(SUBLANE - 1, tm), :]
    x2 = buf_ref[pl.ds(SUBLANE - 2, tm), :]
    if p1 is not None:
        t = lax.broadcasted_iota(jnp.int32, x.shape, 0) % seq_len
        x1 = jnp.where(t == 0, p1, x1)
        x2 = jnp.where(t < 2, p2, x2)
    cw = cw_ref[...]
    return cw[0:1, :] * x2 + cw[1:2, :] * x1 + cw[2:3, :] * x


def _mix_kernel(*refs, sample, seq_len):
    if sample:
        (h_ref, ya_ref, rest_ref, cw_ref, gv_ref, ws_ref, bs_ref, go_ref, wo_ref, p1_ref, p2_ref,
         out_ref, x_ref, vv_ref, buf_ref, y_ref) = refs
    else:
        (h_ref, ya_ref, rest_ref, cw_ref, gv_ref, ws_ref, bs_ref, go_ref, wo_ref,
         out_ref, x_ref, carry_ref, buf_ref, y_ref) = refs
    tm = h_ref.shape[0]
    i = pl.program_id(0)

    bg = rest_ref[:, 0:W_B]
    x = rest_ref[:, W_B:2 * W_B] * rest_ref[:, 2 * W_B:3 * W_B]
    x_ref[...] = x
    if sample:
        conv = _conv3_rows(x, cw_ref, buf_ref, jnp.zeros((SUBLANE, W_B), F32),
                           p1_ref[...], p2_ref[...], seq_len)
    else:
        @pl.when(i == 0)
        def _():
            carry_ref[...] = jnp.zeros_like(carry_ref)
        conv = _conv3_rows(x, cw_ref, buf_ref, carry_ref[...], None, None, seq_len)
        carry_ref[...] = x[tm - SUBLANE:, :]
    yb = bg * conv

    go = go_ref[...]

    def put_head(hidx, yh):
        ms = jnp.mean(yh * yh, axis=-1, keepdims=True)
        lo = hidx * HEAD_DIM
        y_ref[:, lo:lo + HEAD_DIM] = (yh * lax.rsqrt(ms + EPS) * go[:, lo:lo + HEAD_DIM]).astype(BF16)

    for hh in range(H_A):
        put_head(hh, ya_ref[:, hh * HEAD_DIM:(hh + 1) * HEAD_DIM])
    for hh in range(W_B // HEAD_DIM):
        put_head(H_A + hh, yb[:, hh * HEAD_DIM:(hh + 1) * HEAD_DIM])

    gv = gv_ref[...]
    bs = bs_ref[...]
    for hh in range(H_C):
        lo = hh * HEAD_DIM
        u = _gelu_tanh(rest_ref[:, 3 * W_B + lo:3 * W_B + lo + HEAD_DIM])
        gvc = _gelu_tanh(rest_ref[:, 3 * W_B + W_C + lo:3 * W_B + W_C + lo + HEAD_DIM])
        ms = jnp.mean(gvc * gvc, axis=-1, keepdims=True)
        vv = gvc * lax.rsqrt(ms + EPS) * gv[:, lo:lo + HEAD_DIM]
        if sample:
            vv_ref[:, lo:lo + HEAD_DIM] = vv
        vvb = vv.astype(BF16)
        w = ws_ref[hh]
        parts = []
        for c in range(tm // CHUNK):
            parts.append(jnp.dot(w, vvb[c * CHUNK:(c + 1) * CHUNK, :], preferred_element_type=F32)
                         + bs[:, lo:lo + HEAD_DIM])
        s = parts[0] if len(parts) == 1 else jnp.concatenate(parts, axis=0)
        put_head(H_A + W_B // HEAD_DIM + hh, u * s)

    out_ref[...] = h_ref[...] + jnp.dot(y_ref[...], wo_ref[...], preferred_element_type=F32)


def _mix(h, ya, rest, cw, g_v, ws, bs, g_out, w_o, tm, p1=None, p2=None, seq_len=1):
    m = h.shape[0]
    sample = p1 is not None
    row = lambda i: (i, 0)
    fixed = lambda i: (0, 0)
    in_specs = [
        pl.BlockSpec((tm, D_MODEL), row),
        pl.BlockSpec((tm, W_A), row),
        pl.BlockSpec((tm, REST_W), row),
        pl.BlockSpec((3, W_B), fixed),
        pl.BlockSpec((1, W_C), fixed),
        pl.BlockSpec((H_C, CHUNK, CHUNK), lambda i: (0, 0, 0)),
        pl.BlockSpec((CHUNK, W_C), fixed),
        pl.BlockSpec((1, D_MODEL), fixed),
        pl.BlockSpec((D_MODEL, D_MODEL), fixed),
    ]
    args = [h, ya, rest, cw, g_v, ws, bs, g_out, w_o]
    out_specs = [pl.BlockSpec((tm, D_MODEL), row), pl.BlockSpec((tm, W_B), row)]
    out_shape = [jax.ShapeDtypeStruct((m, D_MODEL), F32), jax.ShapeDtypeStruct((m, W_B), F32)]
    scratch = []
    if sample:
        in_specs += [pl.BlockSpec((tm, W_B), row), pl.BlockSpec((tm, W_B), row)]
        args += [p1, p2]
        out_specs.append(pl.BlockSpec((tm, W_C), row))
        out_shape.append(jax.ShapeDtypeStruct((m, W_C), F32))
    else:
        scratch.append(pltpu.VMEM((SUBLANE, W_B), F32))
    scratch += [pltpu.VMEM((tm + SUBLANE, W_B), F32), pltpu.VMEM((tm, D_MODEL), BF16)]
    return pl.pallas_call(
        functools.partial(_mix_kernel, sample=sample, seq_len=seq_len),
        grid=(m // tm,),
        in_specs=in_specs,
        out_specs=out_specs,
        out_shape=out_shape,
        scratch_shapes=scratch,
        compiler_params=_cparams("arbitrary"),
        name="mix_sample" if sample else "mix_prompt",
    )(*args)


def _ffn_kernel(*refs, sample, seq_len):
    if sample:
        (h_ref, g_ref, wg_ref, wv_ref, cwg_ref, cwv_ref, wd_ref, p1g_ref, p2g_ref, p1v_ref, p2v_ref,
         out_ref, upg_ref, upv_ref, xn_ref, acc_ref, buf_ref) = refs
    else:
        (h_ref, g_ref, wg_ref, wv_ref, cwg_ref, cwv_ref, wd_ref,
         out_ref, upg_ref, upv_ref, xn_ref, acc_ref, buf_ref, carry_ref) = refs
    tm = h_ref.shape[0]
    i = pl.program_id(0)
    j = pl.program_id(1)

    @pl.when(j == 0)
    def _():
        xn_ref[...] = _rms_rows(h_ref[...], g_ref[...]).astype(BF16)
        acc_ref[...] = jnp.zeros_like(acc_ref)

    if not sample:
        @pl.when(jnp.logical_and(i == 0, j == 0))
        def _():
            carry_ref[...] = jnp.zeros_like(carry_ref)

    def branch(w_ref, cw_ref, idx, up_ref, p1_ref, p2_ref):
        up = jnp.dot(xn_ref[...], w_ref[...], preferred_element_type=F32)
        if sample:
            up_ref[...] = up
            return _conv3_rows(up, cw_ref, buf_ref, jnp.zeros((SUBLANE, FF_TILE), F32),
                               p1_ref[...], p2_ref[...], seq_len)
        conv = _conv3_rows(up, cw_ref, buf_ref, carry_ref[idx, j], None, None, seq_len)
        last = up[tm - SUBLANE:, :]
        carry_ref[idx, j] = last
        up_ref[...] = last
        return conv

    if sample:
        gc = branch(wg_ref, cwg_ref, 0, upg_ref, p1g_ref, p2g_ref)
        vc = branch(wv_ref, cwv_ref, 1, upv_ref, p1v_ref, p2v_ref)
    else:
        gc = branch(wg_ref, cwg_ref, 0, upg_ref, None, None)
        vc = branch(wv_ref, cwv_ref, 1, upv_ref, None, None)
    act = gc * (1.0 / (1.0 + jnp.exp(-gc))) * vc
    acc_ref[...] += jnp.dot(act.astype(BF16), wd_ref[...], preferred_element_type=F32)

    @pl.when(j == pl.num_programs(1) - 1)
    def _():
        out_ref[...] = h_ref[...] + acc_ref[...]


def _ffn(h, g, wg, wv, cwg, cwv, wd, tm, prevs=None, seq_len=1):
    m = h.shape[0]
    sample = prevs is not None
    nj = D_FF_PAD // FF_TILE
    row = lambda i, j: (i, 0)
    col = lambda i, j: (0, j)
    in_specs = [
        pl.BlockSpec((tm, D_MODEL), row),
        pl.BlockSpec((1, D_MODEL), lambda i, j: (0, 0)),
        pl.BlockSpec((D_MODEL, FF_TILE), col),
        pl.BlockSpec((D_MODEL, FF_TILE), col),
        pl.BlockSpec((3, FF_TILE), col),
        pl.BlockSpec((3, FF_TILE), col),
        pl.BlockSpec((FF_TILE, D_MODEL), lambda i, j: (j, 0)),
    ]
    args = [h, g, wg, wv, cwg, cwv, wd]
    up_rows = tm if sample else SUBLANE
    up_spec = pl.BlockSpec((up_rows, FF_TILE), lambda i, j: (i, j))
    up_shape = jax.ShapeDtypeStruct((m // tm * up_rows, D_FF_PAD), F32)
    out_specs = [pl.BlockSpec((tm, D_MODEL), row), up_spec, up_spec]
    out_shape = [jax.ShapeDtypeStruct((m, D_MODEL), F32), up_shape, up_shape]
    scratch = [pltpu.VMEM((tm, D_MODEL), BF16), pltpu.VMEM((tm, D_MODEL), F32),
               pltpu.VMEM((tm + SUBLANE, FF_TILE), F32)]
    if sample:
        in_specs += [pl.BlockSpec((tm, FF_TILE), col)] * 4
        args += list(prevs)
    else:
        scratch.append(pltpu.VMEM((2, nj, SUBLANE, FF_TILE), F32))
    return pl.pallas_call(
        functools.partial(_ffn_kernel, sample=sample, seq_len=seq_len),
        grid=(m // tm, nj),
        in_specs=in_specs,
        out_specs=out_specs,
        out_shape=out_shape,
        scratch_shapes=scratch,
        compiler_params=_cparams("arbitrary", "arbitrary"),
        name="ffn_sample" if sample else "ffn_prompt",
    )(*args)


def _ple_kernel(h_ref, g_ref, p_ref, wpg_ref, wpp_ref, gf_ref, out_ref, *, final):
    h = h_ref[...]
    xn = _rms_rows(h, g_ref[...]).astype(BF16)
    gate = jnp.dot(xn, wpg_ref[...], preferred_element_type=F32)
    gate = 1.0 / (1.0 + jnp.exp(-gate))
    proj = jnp.dot(p_ref[...].astype(BF16), wpp_ref[...], preferred_element_type=F32)
    hn = h + gate * proj
    if final:
        hn = _rms_rows(hn, gf_ref[...])
    out_ref[...] = hn


def _ple(h, g, p, wpg, wpp, g_final, tm, final):
    m = h.shape[0]
    row = lambda i: (i, 0)
    fixed = lambda i: (0, 0)
    return pl.pallas_call(
        functools.partial(_ple_kernel, final=final),
        grid=(m // tm,),
        in_specs=[
            pl.BlockSpec((tm, D_MODEL), row),
            pl.BlockSpec((1, D_MODEL), fixed),
            pl.BlockSpec((tm, PLE_DIM), row),
            pl.BlockSpec((D_MODEL, D_MODEL), fixed),
            pl.BlockSpec((PLE_DIM, D_MODEL), fixed),
            pl.BlockSpec((1, D_MODEL), fixed),
        ],
        out_specs=pl.BlockSpec((tm, D_MODEL), row),
        out_shape=jax.ShapeDtypeStruct((m, D_MODEL), F32),
        compiler_params=_cparams("arbitrary"),
        name="ple",
    )(h, g, p, wpg, wpp, g_final)


def _expand_state(state, seq_len):
    b, _, c = state.shape
    zeros = jnp.zeros((b, seq_len - 1, c), state.dtype)
    p1 = jnp.concatenate([state[:, 1:2], zeros], axis=1).reshape(b * seq_len, c)
    p2 = jnp.concatenate([state[:, 0:1], state[:, 1:2], zeros[:, 1:]], axis=1).reshape(b * seq_len, c)
    return p1, p2


def _pad_ff(x):
    return jnp.pad(x, ((0, 0), (0, D_FF_PAD - D_FF)))


def kernel(x_prompt, x_sample, cache_k, cache_v, state_conv_mix, state_conv_ffn, page_table,
           p_prompt, p_sample, g_mix, w_in, b_sb, conv_mix_w, g_v, w_s, b_s, g_out, w_o,
           g_ffn, w_up, conv_ffn_w, w_down, g_ple, w_ple_gate, w_ple_proj, g_final):
    depth = w_in.shape[0]
    bp, seq, _ = x_prompt.shape
    n_seq, n_q, _ = x_sample.shape
    assert bp == 1 and n_q >= 2 and n_q * n_seq == CHUNK
    ms = n_seq * n_q
    hp = x_prompt.reshape(seq, D_MODEL)
    hs = x_sample.reshape(ms, D_MODEL)
    gfin = g_final.reshape(1, D_MODEL)
    tril = jnp.tril(jnp.ones((CHUNK, CHUNK), F32))

    outs = {k: [] for k in ("kp", "vp", "cbp", "cfp", "ks", "vs", "cbs", "cfs", "cvs")}
    for l in range(depth):
        w_in_b = w_in[l].astype(BF16)
        w_o_b = w_o[l].astype(BF16)
        wg_b = _pad_ff(w_up[l][:, :D_FF]).astype(BF16)
        wv_b = _pad_ff(w_up[l][:, D_FF:]).astype(BF16)
        cwg = _pad_ff(conv_ffn_w[l][:, :D_FF])
        cwv = _pad_ff(conv_ffn_w[l][:, D_FF:])
        wd_b = jnp.pad(w_down[l], ((0, D_FF_PAD - D_FF), (0, 0))).astype(BF16)
        wpg_b = w_ple_gate[l].astype(BF16)
        wpp_b = w_ple_proj[l].astype(BF16)
        gm = g_mix[l].reshape(1, D_MODEL)
        gv = g_v[l].reshape(1, W_C)
        go = g_out[l].reshape(1, D_MODEL)
        gf = g_ffn[l].reshape(1, D_MODEL)
        gp = g_ple[l].reshape(1, D_MODEL)
        final = l == depth - 1

        ws_p = (w_s[l] * tril).astype(BF16)
        bs_p = jnp.repeat(b_s[l].T, HEAD_DIM, axis=1)
        wm_q = (w_s[l] * tril)[:, :n_q, :n_q]
        ws_s = jnp.einsum("ab,hts->hatbs", jnp.eye(n_seq, dtype=F32), wm_q).reshape(H_C, ms, ms).astype(BF16)
        bs_s = jnp.repeat(jnp.tile(b_s[l][:, :n_q].T, (n_seq, 1)), HEAD_DIM, axis=1)

        qkv, kp, vp, rest = _in_proj(hp, gm, w_in_b, 1024, head_rows=True)
        ya = _attention_prompt(qkv, b_sb[l])
        hp, xcp = _mix(hp, ya, rest, conv_mix_w[l], gv, ws_p, bs_p, go, w_o_b, 256)
        hp, cfg, cfv = _ffn(hp, gf, wg_b, wv_b, cwg, cwv, wd_b, 512)
        hp = _ple(hp, gp, p_prompt[l].reshape(seq, PLE_DIM), wpg_b, wpp_b, gfin, 512, final)
        outs["kp"].append(kp.reshape(1, seq, H_A, HEAD_DIM))
        outs["vp"].append(vp.reshape(1, seq, H_A, HEAD_DIM))
        outs["cbp"].append(xcp[seq - 2:].reshape(1, 2, W_B))
        outs["cfp"].append(jnp.concatenate([cfg[-2:, :D_FF], cfv[-2:, :D_FF]],
                                           axis=-1).reshape(1, 2, 2 * D_FF))

        qkv_s, ks, vs, rest_s = _in_proj(hs, gm, w_in_b, ms, head_rows=False)
        ya_s = _attention_sample(qkv_s, ks, vs, b_sb[l], cache_k, cache_v, page_table, l)
        p1m, p2m = _expand_state(state_conv_mix[l], n_q)
        hs, xcs, vvs = _mix(hs, ya_s, rest_s, conv_mix_w[l], gv, ws_s, bs_s, go, w_o_b, ms,
                            p1=p1m, p2=p2m, seq_len=n_q)
        p1g, p2g = _expand_state(_pad_ff(state_conv_ffn[l][..., :D_FF].reshape(n_seq * 2, D_FF))
                                 .reshape(n_seq, 2, D_FF_PAD), n_q)
        p1v, p2v = _expand_state(_pad_ff(state_conv_ffn[l][..., D_FF:].reshape(n_seq * 2, D_FF))
                                 .reshape(n_seq, 2, D_FF_PAD), n_q)
        hs, upg, upv = _ffn(hs, gf, wg_b, wv_b, cwg, cwv, wd_b, ms,
                            prevs=(p1g, p2g, p1v, p2v), seq_len=n_q)
        hs = _ple(hs, gp, p_sample[l].reshape(ms, PLE_DIM), wpg_b, wpp_b, gfin, ms, final)
        outs["ks"].append(ks.reshape(n_seq, n_q, H_A, HEAD_DIM))
        outs["vs"].append(vs.reshape(n_seq, n_q, H_A, HEAD_DIM))
        outs["cbs"].append(xcs.reshape(n_seq, n_q, W_B)[:, n_q - 2:])
        up_s = jnp.concatenate([upg[:, :D_FF], upv[:, :D_FF]], axis=-1)
        outs["cfs"].append(up_s.reshape(n_seq, n_q, 2 * D_FF)[:, n_q - 2:])
        outs["cvs"].append(vvs.reshape(n_seq, n_q, W_C))

    st = lambda k: jnp.stack(outs[k])
    return (hp.reshape(1, seq, D_MODEL), hs.reshape(n_seq, n_q, D_MODEL),
            st("kp"), st("vp"), st("cbp"), st("cfp"),
            st("ks"), st("vs"), st("cbs"), st("cfs"), st("cvs"))
```

```python
import functools

import numpy as np
import jax
import jax.numpy as jnp
from jax import lax
from jax.experimental import pallas as pl
from jax.experimental.pallas import tpu as pltpu

F32 = jnp.float32
BF16 = jnp.bfloat16

D_MODEL = 2048
HEAD_DIM = 128
W_A = 1024
W_B = 512
W_C = 512
H_A = W_A // HEAD_DIM
H_C = W_C // HEAD_DIM
N_HEADS = D_MODEL // HEAD_DIM
IN_W = 3 * W_A + 3 * W_B + 2 * W_C
REST_W = IN_W - 3 * W_A
D_FF = 5504
CHUNK = 128
PAGE = 128
PLE_DIM = 256
EPS = 1e-6
SCALE = HEAD_DIM ** -0.5
LOG2E = float(np.log2(np.e))
Q_PRESCALE = SCALE * LOG2E

LANE = 128
SUBLANE = 8
FF_TILE = 512
D_FF_PAD = ((D_FF + FF_TILE - 1) // FF_TILE) * FF_TILE
IN_TILE = 512
VMEM_LIMIT = 56 * 1024 * 1024


def _cparams(*sem):
    return pltpu.CompilerParams(dimension_semantics=sem, vmem_limit_bytes=VMEM_LIMIT)


def _rms_rows(x, g):
    ms = jnp.mean(x * x, axis=-1, keepdims=True)
    return x * lax.rsqrt(ms + EPS) * g


def _gelu_tanh(x):
    c = np.float32(np.sqrt(2.0 / np.pi))
    return 0.5 * x * (1.0 + jnp.tanh(c * (x + 0.044715 * (x * x * x))))


def _softplus2(t):
    return jnp.maximum(t, 0.0) + jnp.log2(1.0 + jnp.exp2(-jnp.abs(t)))


def _split_bf16(x):
    hi = x.astype(BF16)
    lo = (x - hi.astype(F32)).astype(BF16)
    return hi, lo


def _in_proj_kernel(x_ref, g_ref, w_ref, qkv_ref, k_ref, v_ref, rest_ref, xn_ref, *, head_rows):
    j = pl.program_id(1)
    tm = x_ref.shape[0]

    @pl.when(j == 0)
    def _():
        xn_ref[...] = _rms_rows(x_ref[...], g_ref[...]).astype(BF16)

    r = jnp.dot(xn_ref[...], w_ref[...], preferred_element_type=F32)
    nq = W_A // IN_TILE
    heads_per_tile = IN_TILE // HEAD_DIM

    @pl.when(j < nq)
    def _():
        qkv_ref[...] = (r * Q_PRESCALE).astype(BF16)

    @pl.when(jnp.logical_and(j >= nq, j < 3 * nq))
    def _():
        qkv_ref[...] = r.astype(BF16)

    def put_f32(out_ref, first_tile):
        if not head_rows:
            @pl.when(jnp.logical_and(j >= first_tile, j < first_tile + nq))
            def _():
                out_ref[...] = r
            return
        for jj in range(nq):
            @pl.when(j == first_tile + jj)
            def _(jj=jj):
                for hh in range(heads_per_tile):
                    out_ref[pl.ds(jj * heads_per_tile + hh, tm, stride=H_A), :] = (
                        r[:, hh * HEAD_DIM:(hh + 1) * HEAD_DIM])

    put_f32(k_ref, nq)
    put_f32(v_ref, 2 * nq)

    @pl.when(j >= 3 * nq)
    def _():
        rest_ref[...] = r


def _in_proj(x, g, w, layer, tm, head_rows):
    m = x.shape[0]
    nq = W_A // IN_TILE
    nj = IN_W // IN_TILE
    nrest = REST_W // IN_TILE
    if head_rows:
        kv_spec = lambda first: pl.BlockSpec((tm * H_A, HEAD_DIM), lambda i, j: (i, 0))
        kv_shape = jax.ShapeDtypeStruct((m * H_A, HEAD_DIM), F32)
    else:
        kv_spec = lambda first: pl.BlockSpec((tm, IN_TILE), lambda i, j: (i, jnp.clip(j - first, 0, nq - 1)))
        kv_shape = jax.ShapeDtypeStruct((m, W_A), F32)
    return pl.pallas_call(
        functools.partial(_in_proj_kernel, head_rows=head_rows),
        grid=(m // tm, nj),
        in_specs=[
            pl.BlockSpec((tm, D_MODEL), lambda i, j: (i, 0)),
            pl.BlockSpec((1, D_MODEL), lambda i, j: (0, 0)),
            pl.BlockSpec((None, D_MODEL, IN_TILE), lambda i, j: (layer, 0, j)),
        ],
        out_specs=[
            pl.BlockSpec((tm, IN_TILE), lambda i, j: (i, jnp.minimum(j, 3 * nq - 1))),
            kv_spec(nq),
            kv_spec(2 * nq),
            pl.BlockSpec((tm, IN_TILE), lambda i, j: (i, jnp.clip(j - 3 * nq, 0, nrest - 1))),
        ],
        out_shape=[
            jax.ShapeDtypeStruct((m, 3 * W_A), BF16),
            kv_shape,
            kv_shape,
            jax.ShapeDtypeStruct((m, REST_W), F32),
        ],
        scratch_shapes=[pltpu.VMEM((tm, D_MODEL), BF16)],
        compiler_params=_cparams("arbitrary", "arbitrary"),
        name="in_proj",
    )(x, g, w)


ATT_TQ = 256
ATT_TK = 128
ATT_SPAN = 512
ATT_HEADS = 4
ATT_UNIT = 2 * ATT_TK
ATT_QK_SKEW = 2
ATT_SKEW = 2
assert ATT_SPAN % ATT_TQ == 0 and H_A % ATT_HEADS == 0


def _cumsum_matrix():
    j = np.arange(ATT_UNIT)[:, None]
    s = np.arange(ATT_UNIT)[None, :]
    return jnp.asarray((j > s).astype(np.float32), dtype=BF16)


def _attn_kernel(b_ref, q_ref, k_ref, v_ref, u_ref, o_ref, acc_ref, carry_ref):
    i = pl.program_id(1)
    acc_ref[...] = jnp.zeros_like(acc_ref)
    carry_ref[...] = jnp.zeros_like(carry_ref)
    heads = range(ATT_HEADS)
    lanes = [slice(e * HEAD_DIM, (e + 1) * HEAD_DIM) for e in heads]
    bias2 = [b_ref[pl.program_id(0) * ATT_HEADS + e] * LOG2E for e in heads]

    def span(start, n_keys, masked):
        start = pl.multiple_of(start, ATT_SPAN)
        units = [(e, u) for u in range(n_keys // ATT_UNIT - 1, -1, -1) for e in heads]
        cs = [carry_ref[e] for e in heads]
        accs = [None for _ in heads]
        logits = {}
        pending = {}

        def stage_qk(e, u):
            k = k_ref[pl.ds(start + u * ATT_UNIT, ATT_UNIT), lanes[e]]
            logits[(e, u)] = lax.dot_general(q_ref[:, lanes[e]], k, (((1,), (1,)), ((), ())),
                                             preferred_element_type=F32) + bias2[e]

        def stage_scores(e, u):
            t = logits.pop((e, u))
            sp = _softplus2(t)
            valid = None
            if masked:
                row = lax.broadcasted_iota(jnp.int32, (ATT_TQ, ATT_UNIT), 0) + i * ATT_TQ
                col = lax.broadcasted_iota(jnp.int32, (ATT_TQ, ATT_UNIT), 1) + (start + u * ATT_UNIT)
                valid = col < row
                sp = jnp.where(valid, sp, 0.0)
            later = jnp.dot(sp.astype(BF16), u_ref[...], preferred_element_type=F32)
            total = jnp.broadcast_to(jnp.sum(sp, axis=1, keepdims=True), (ATT_TQ, ATT_TK))
            pending[(e, u)] = (t - sp - later, total, valid)

        def stage_out(e, u):
            x, total, valid = pending.pop((e, u))
            a = jnp.exp2(x - jnp.concatenate([cs[e]] * (ATT_UNIT // ATT_TK), axis=1))
            if masked:
                a = jnp.where(valid, a, 0.0)
            cs[e] = cs[e] + total
            v = v_ref[pl.ds(start + u * ATT_UNIT, ATT_UNIT), lanes[e]]
            o = jnp.dot(a.astype(BF16), v, preferred_element_type=F32)
            accs[e] = o if accs[e] is None else accs[e] + o

        n_units = len(units)
        for idx in range(n_units + ATT_QK_SKEW + ATT_SKEW):
            if idx < n_units:
                stage_qk(*units[idx])
            if 0 <= idx - ATT_QK_SKEW < n_units:
                stage_scores(*units[idx - ATT_QK_SKEW])
            if 0 <= idx - ATT_QK_SKEW - ATT_SKEW < n_units:
                stage_out(*units[idx - ATT_QK_SKEW - ATT_SKEW])
        for e in heads:
            carry_ref[e] = cs[e]
            acc_ref[e] += accs[e]

    n_full = (i * ATT_TQ) // ATT_SPAN
    span(n_full * ATT_SPAN, ATT_SPAN, True)

    @pl.when(n_full % 2 == 1)
    def _():
        span((n_full - 1) * ATT_SPAN, ATT_SPAN, False)

    n_pairs = n_full // 2

    def body(it, c):
        span((n_pairs - 1 - it) * (2 * ATT_SPAN), 2 * ATT_SPAN, False)
        return c

    lax.fori_loop(0, n_pairs, body, 0)
    for e in heads:
        o_ref[:, lanes[e]] = acc_ref[e]


def _attention_prompt(qkv, b_sb):
    s = qkv.shape[0]
    width = ATT_HEADS * HEAD_DIM
    groups = H_A // ATT_HEADS
    return pl.pallas_call(
        _attn_kernel,
        grid=(groups, s // ATT_TQ),
        in_specs=[
            pl.BlockSpec(memory_space=pltpu.SMEM),
            pl.BlockSpec((ATT_TQ, width), lambda h, i: (i, h)),
            pl.BlockSpec((s, width), lambda h, i: (0, groups + h)),
            pl.BlockSpec((s, width), lambda h, i: (0, 2 * groups + h)),
            pl.BlockSpec((ATT_UNIT, ATT_UNIT), lambda h, i: (0, 0)),
        ],
        out_specs=pl.BlockSpec((ATT_TQ, width), lambda h, i: (i, h)),
        out_shape=jax.ShapeDtypeStruct((s, W_A), F32),
        scratch_shapes=[pltpu.VMEM((ATT_HEADS, ATT_TQ, HEAD_DIM), F32),
                        pltpu.VMEM((ATT_HEADS, ATT_TQ, ATT_TK), F32)],
        compiler_params=_cparams("arbitrary", "arbitrary"),
        name="attn_prompt",
    )(b_sb, qkv, qkv, qkv, _cumsum_matrix())


SCOLS = LANE


SATT_PAGES = 8


def _sattn_kernel(pt_ref, qbd_ref, knew_ref, vnew_ref, *refs, n_q):
    del pt_ref
    kp_refs = refs[:SATT_PAGES]
    vp_refs = refs[SATT_PAGES:2 * SATT_PAGES]
    brow_ref, nmask_ref, l_ref, hm_ref, o_ref, acc_ref, carry_ref, new_ref = refs[2 * SATT_PAGES:]
    p = pl.program_id(1)
    qbd = qbd_ref[...]
    brow = brow_ref[...]
    nrow = n_q * H_A

    def heads_to_lanes(ref):
        return jnp.concatenate(
            [ref[pl.ds(hh, PAGE, stride=H_A), :].astype(BF16) for hh in range(H_A)], axis=1)

    lmat = l_ref[...]

    def page_scores(kb, mask):
        zt = jnp.dot(kb, qbd, preferred_element_type=F32) + brow
        sp = _softplus2(zt)
        if mask is not None:
            sp = jnp.where(mask > 0.0, sp, 0.0)
        hi, lo = _split_bf16(sp)
        later = (jnp.dot(lmat, hi, preferred_element_type=F32)
                 + jnp.dot(lmat, lo, preferred_element_type=F32))
        return zt - sp - later, jnp.sum(sp, axis=0, keepdims=True)

    def page_out(x, c, vb, mask):
        a = jnp.exp2(x - c)
        if mask is not None:
            a = jnp.where(mask > 0.0, a, 0.0)
        at = a.T[:nrow].astype(BF16)
        return jnp.dot(at, vb, preferred_element_type=F32)

    @pl.when(p == 0)
    def _():
        mask = nmask_ref[...]
        new_ref[...] = jnp.zeros_like(new_ref)
        new_ref[0, 0:n_q, :] = knew_ref[...]
        new_ref[1, 0:n_q, :] = vnew_ref[...]
        x, tot = page_scores(new_ref[0].astype(BF16), mask)
        acc_ref[...] = page_out(x, jnp.zeros_like(tot), new_ref[1].astype(BF16), mask)
        carry_ref[...] = tot

    pages = range(SATT_PAGES)
    zts = [jnp.dot(heads_to_lanes(r), qbd, preferred_element_type=F32) + brow for r in kp_refs]
    sps = [_softplus2(zt) for zt in zts]
    splits = [_split_bf16(sp) for sp in sps]
    laters = [jnp.dot(lmat, hi, preferred_element_type=F32) + jnp.dot(lmat, lo, preferred_element_type=F32)
              for hi, lo in splits]
    tots = [jnp.sum(sp, axis=0, keepdims=True) for sp in sps]
    cs = [None] * SATT_PAGES
    c = carry_ref[...]
    for g in range(SATT_PAGES - 1, -1, -1):
        cs[g] = c
        c = c + tots[g]
    carry_ref[...] = c
    ats = [jnp.exp2(zts[g] - sps[g] - laters[g] - cs[g]).T[:nrow].astype(BF16) for g in pages]
    outs = [jnp.dot(ats[g], heads_to_lanes(vp_refs[g]), preferred_element_type=F32) for g in pages]
    acc_ref[...] += sum(outs[1:], outs[0])

    @pl.when(p == pl.num_programs(1) - 1)
    def _():
        res = acc_ref[...] * hm_ref[...]
        o_ref[...] = jnp.sum(res.reshape(n_q, H_A, W_A), axis=1)


def _attention_sample(qkv, k_new, v_new, b_sb, cache_k, cache_v, page_table, layer):
    n_seq, n_pages = page_table.shape
    n_q = qkv.shape[0] // n_seq
    nrow = n_q * H_A
    q = qkv[:, :W_A].reshape(n_seq, n_q, H_A, HEAD_DIM)
    eye = jnp.eye(H_A, dtype=BF16)
    qbd = jnp.einsum("bthd,hg->bhdtg", q, eye).reshape(n_seq, W_A, nrow)
    qbd = jnp.pad(qbd, ((0, 0), (0, 0), (0, SCOLS - nrow)))
    knew = k_new.reshape(n_seq, n_q, W_A)
    vnew = v_new.reshape(n_seq, n_q, W_A)
    cols = np.arange(SCOLS)
    brow = jnp.where(cols < nrow, b_sb[cols % H_A] * LOG2E, 0.0).reshape(1, SCOLS).astype(F32)
    keys = np.arange(PAGE)[:, None]
    nmask = jnp.asarray(((cols[None, :] < nrow) & (keys < cols[None, :] // H_A)).astype(np.float32))
    lmat = jnp.asarray((np.arange(PAGE)[None, :] > keys).astype(np.float32), dtype=BF16)
    hm = jnp.asarray((np.arange(W_A)[None, :] // HEAD_DIM
                      == np.arange(nrow)[:, None] % H_A).astype(np.float32))
    n_pool = cache_k.shape[1]
    ck = cache_k.reshape(cache_k.shape[0], n_pool, PAGE * H_A, HEAD_DIM)
    cv = cache_v.reshape(cache_v.shape[0], n_pool, PAGE * H_A, HEAD_DIM)
    assert n_pages % SATT_PAGES == 0
    n_steps = n_pages // SATT_PAGES

    def page_spec(r):
        return pl.BlockSpec(
            (None, None, PAGE * H_A, HEAD_DIM),
            lambda b, p, pt: (layer, pt[b, n_pages - (p + 1) * SATT_PAGES + r], 0, 0))

    fixed = lambda b, p, pt: (0, 0)
    page_specs = [page_spec(r) for r in range(SATT_PAGES)]
    grid_spec = pltpu.PrefetchScalarGridSpec(
        num_scalar_prefetch=1,
        grid=(n_seq, n_steps),
        in_specs=[
            pl.BlockSpec((None, W_A, SCOLS), lambda b, p, pt: (b, 0, 0)),
            pl.BlockSpec((None, n_q, W_A), lambda b, p, pt: (b, 0, 0)),
            pl.BlockSpec((None, n_q, W_A), lambda b, p, pt: (b, 0, 0)),
            *page_specs, *page_specs,
            pl.BlockSpec((1, SCOLS), fixed),
            pl.BlockSpec((PAGE, SCOLS), fixed),
            pl.BlockSpec((PAGE, PAGE), fixed),
            pl.BlockSpec((nrow, W_A), fixed),
        ],
        out_specs=pl.BlockSpec((None, n_q, W_A), lambda b, p, pt: (b, 0, 0)),
        scratch_shapes=[pltpu.VMEM((nrow, W_A), F32), pltpu.VMEM((1, SCOLS), F32),
                        pltpu.VMEM((2, PAGE, W_A), F32)],
    )
    out = pl.pallas_call(
        functools.partial(_sattn_kernel, n_q=n_q),
        grid_spec=grid_spec,
        out_shape=jax.ShapeDtypeStruct((n_seq, n_q, W_A), F32),
        compiler_params=_cparams("arbitrary", "arbitrary"),
        name="attn_sample",
    )(page_table, qbd, knew, vnew, *([ck] * SATT_PAGES), *([cv] * SATT_PAGES), brow, nmask, lmat, hm)
    return out.reshape(n_seq * n_q, W_A)


def _conv3_rows(x, cw_ref, buf_ref, prev8, p1, p2, seq_len):
    tm = x.shape[0]
    buf_ref[pl.ds(0, SUBLANE), :] = prev8
    buf_ref[pl.ds(SUBLANE, tm), :] = x
    x1 = buf_ref[pl.ds(SUBLANE - 1, tm), :]
    x2 = buf_ref[pl.ds(SUBLANE - 2, tm), :]
    if p1 is not None:
        t = lax.broadcasted_iota(jnp.int32, x.shape, 0) % seq_len
        x1 = jnp.where(t == 0, p1, x1)
        x2 = jnp.where(t < 2, p2, x2)
    cw = cw_ref[...]
    return cw[0:1, :] * x2 + cw[1:2, :] * x1 + cw[2:3, :] * x


def _mix_kernel(*refs, sample, seq_len):
    if sample:
        (h_ref, ya_ref, rest_ref, cw_ref, gv_ref, ws_ref, bs_ref, go_ref, wo_ref, p1_ref, p2_ref,
         out_ref, x_ref, vv_ref, buf_ref, y_ref) = refs
    else:
        (h_ref, ya_ref, rest_ref, cw_ref, gv_ref, ws_ref, bs_ref, go_ref, wo_ref,
         out_ref, x_ref, carry_ref, buf_ref, y_ref) = refs
    tm = h_ref.shape[0]
    i = pl.program_id(0)

    bg = rest_ref[:, 0:W_B]
    x = rest_ref[:, W_B:2 * W_B] * rest_ref[:, 2 * W_B:3 * W_B]
    x_ref[...] = x
    if sample:
        conv = _conv3_rows(x, cw_ref, buf_ref, jnp.zeros((SUBLANE, W_B), F32),
                           p1_ref[...], p2_ref[...], seq_len)
    else:
        @pl.when(i == 0)
        def _():
            carry_ref[...] = jnp.zeros_like(carry_ref)
        conv = _conv3_rows(x, cw_ref, buf_ref, carry_ref[...], None, None, seq_len)
        carry_ref[...] = x[tm - SUBLANE:, :]
    yb = bg * conv

    go = go_ref[...]

    def put_head(hidx, yh):
        ms = jnp.mean(yh * yh, axis=-1, keepdims=True)
        lo = hidx * HEAD_DIM
        y_ref[:, lo:lo + HEAD_DIM] = (yh * lax.rsqrt(ms + EPS) * go[:, lo:lo + HEAD_DIM]).astype(BF16)

    for hh in range(H_A):
        put_head(hh, ya_ref[:, hh * HEAD_DIM:(hh + 1) * HEAD_DIM])
    for hh in range(W_B // HEAD_DIM):
        put_head(H_A + hh, yb[:, hh * HEAD_DIM:(hh + 1) * HEAD_DIM])

    gv = gv_ref[...]
    bs = bs_ref[...]
    for hh in range(H_C):
        lo = hh * HEAD_DIM
        u = _gelu_tanh(rest_ref[:, 3 * W_B + lo:3 * W_B + lo + HEAD_DIM])
        gvc = _gelu_tanh(rest_ref[:, 3 * W_B + W_C + lo:3 * W_B + W_C + lo + HEAD_DIM])
        ms = jnp.mean(gvc * gvc, axis=-1, keepdims=True)
        vv = gvc * lax.rsqrt(ms + EPS) * gv[:, lo:lo + HEAD_DIM]
        if sample:
            vv_ref[:, lo:lo + HEAD_DIM] = vv
        vvb = vv.astype(BF16)
        w = ws_ref[hh]
        parts = []
        for c in range(tm // CHUNK):
            parts.append(jnp.dot(w, vvb[c * CHUNK:(c + 1) * CHUNK, :], preferred_element_type=F32)
                         + bs[:, lo:lo + HEAD_DIM])
        s = parts[0] if len(parts) == 1 else jnp.concatenate(parts, axis=0)
        put_head(H_A + W_B // HEAD_DIM + hh, u * s)

    out_ref[...] = h_ref[...] + jnp.dot(y_ref[...], wo_ref[...], preferred_element_type=F32)


def _mix(h, ya, rest, cw, g_v, ws, bs, g_out, w_o, layer, tm, p1=None, p2=None, seq_len=1):
    m = h.shape[0]
    sample = p1 is not None
    row = lambda i: (i, 0)
    fixed = lambda i: (0, 0)
    in_specs = [
        pl.BlockSpec((tm, D_MODEL), row),
        pl.BlockSpec((tm, W_A), row),
        pl.BlockSpec((tm, REST_W), row),
        pl.BlockSpec((3, W_B), fixed),
        pl.BlockSpec((1, W_C), fixed),
        pl.BlockSpec((H_C, CHUNK, CHUNK), lambda i: (0, 0, 0)),
        pl.BlockSpec((CHUNK, W_C), fixed),
        pl.BlockSpec((1, D_MODEL), fixed),
        pl.BlockSpec((None, D_MODEL, D_MODEL), lambda i: (layer, 0, 0)),
    ]
    args = [h, ya, rest, cw, g_v, ws, bs, g_out, w_o]
    out_specs = [pl.BlockSpec((tm, D_MODEL), row), pl.BlockSpec((tm, W_B), row)]
    out_shape = [jax.ShapeDtypeStruct((m, D_MODEL), F32), jax.ShapeDtypeStruct((m, W_B), F32)]
    scratch = []
    if sample:
        in_specs += [pl.BlockSpec((tm, W_B), row), pl.BlockSpec((tm, W_B), row)]
        args += [p1, p2]
        out_specs.append(pl.BlockSpec((tm, W_C), row))
        out_shape.append(jax.ShapeDtypeStruct((m, W_C), F32))
    else:
        scratch.append(pltpu.VMEM((SUBLANE, W_B), F32))
    scratch += [pltpu.VMEM((tm + SUBLANE, W_B), F32), pltpu.VMEM((tm, D_MODEL), BF16)]
    return pl.pallas_call(
        functools.partial(_mix_kernel, sample=sample, seq_len=seq_len),
        grid=(m // tm,),
        in_specs=in_specs,
        out_specs=out_specs,
        out_shape=out_shape,
        scratch_shapes=scratch,
        compiler_params=_cparams("arbitrary"),
        name="mix_sample" if sample else "mix_prompt",
    )(*args)


def _ffn_kernel(*refs, sample, seq_len):
    if sample:
        (h_ref, g_ref, wg_ref, wv_ref, cwg_ref, cwv_ref, wd_ref, p1g_ref, p2g_ref, p1v_ref, p2v_ref,
         out_ref, upg_ref, upv_ref, xn_ref, acc_ref, buf_ref) = refs
    else:
        (h_ref, g_ref, wg_ref, wv_ref, cwg_ref, cwv_ref, wd_ref,
         out_ref, upg_ref, upv_ref, xn_ref, acc_ref, buf_ref, carry_ref) = refs
    tm = h_ref.shape[0]
    i = pl.program_id(0)
    j = pl.program_id(1)

    @pl.when(j == 0)
    def _():
        xn_ref[...] = _rms_rows(h_ref[...], g_ref[...]).astype(BF16)
        acc_ref[...] = jnp.zeros_like(acc_ref)

    if not sample:
        @pl.when(jnp.logical_and(i == 0, j == 0))
        def _():
            carry_ref[...] = jnp.zeros_like(carry_ref)

    def branch(w_ref, cw_ref, idx, up_ref, p1_ref, p2_ref):
        up = jnp.dot(xn_ref[...], w_ref[...], preferred_element_type=F32)
        if sample:
            up_ref[...] = up
            return _conv3_rows(up, cw_ref, buf_ref, jnp.zeros((SUBLANE, FF_TILE), F32),
                               p1_ref[...], p2_ref[...], seq_len)
        conv = _conv3_rows(up, cw_ref, buf_ref, carry_ref[idx, j], None, None, seq_len)
        last = up[tm - SUBLANE:, :]
        carry_ref[idx, j] = last
        up_ref[...] = last
        return conv

    if sample:
        gc = branch(wg_ref, cwg_ref, 0, upg_ref, p1g_ref, p2g_ref)
        vc = branch(wv_ref, cwv_ref, 1, upv_ref, p1v_ref, p2v_ref)
    else:
        gc = branch(wg_ref, cwg_ref, 0, upg_ref, None, None)
        vc = branch(wv_ref, cwv_ref, 1, upv_ref, None, None)
    act = gc * (1.0 / (1.0 + jnp.exp(-gc))) * vc
    acc_ref[...] += jnp.dot(act.astype(BF16), wd_ref[...], preferred_element_type=F32)

    @pl.when(j == pl.num_programs(1) - 1)
    def _():
        out_ref[...] = h_ref[...] + acc_ref[...]


def _ffn(h, g, wg, wv, cwg, cwv, wd, layer, tm, prevs=None, seq_len=1):
    m = h.shape[0]
    sample = prevs is not None
    nj = D_FF_PAD // FF_TILE
    row = lambda i, j: (i, 0)
    col = lambda i, j: (0, j)
    in_specs = [
        pl.BlockSpec((tm, D_MODEL), row),
        pl.BlockSpec((1, D_MODEL), lambda i, j: (0, 0)),
        pl.BlockSpec((None, D_MODEL, FF_TILE), lambda i, j: (layer, 0, j)),
        pl.BlockSpec((None, D_MODEL, FF_TILE), lambda i, j: (layer, 0, j)),
        pl.BlockSpec((3, FF_TILE), col),
        pl.BlockSpec((3, FF_TILE), col),
        pl.BlockSpec((None, FF_TILE, D_MODEL), lambda i, j: (layer, j, 0)),
    ]
    args = [h, g, wg, wv, cwg, cwv, wd]
    up_rows = tm if sample else SUBLANE
    up_spec = pl.BlockSpec((up_rows, FF_TILE), lambda i, j: (i, j))
    up_shape = jax.ShapeDtypeStruct((m // tm * up_rows, D_FF_PAD), F32)
    out_specs = [pl.BlockSpec((tm, D_MODEL), row), up_spec, up_spec]
    out_shape = [jax.ShapeDtypeStruct((m, D_MODEL), F32), up_shape, up_shape]
    scratch = [pltpu.VMEM((tm, D_MODEL), BF16), pltpu.VMEM((tm, D_MODEL), F32),
               pltpu.VMEM((tm + SUBLANE, FF_TILE), F32)]
    if sample:
        in_specs += [pl.BlockSpec((tm, FF_TILE), col)] * 4
        args += list(prevs)
    else:
        scratch.append(pltpu.VMEM((2, nj, SUBLANE, FF_TILE), F32))
    return pl.pallas_call(
        functools.partial(_ffn_kernel, sample=sample, seq_len=seq_len),
        grid=(m // tm, nj),
        in_specs=in_specs,
        out_specs=out_specs,
        out_shape=out_shape,
        scratch_shapes=scratch,
        compiler_params=_cparams("arbitrary", "arbitrary"),
        name="ffn_sample" if sample else "ffn_prompt",
    )(*args)


def _ple_kernel(h_ref, g_ref, p_ref, wpg_ref, wpp_ref, gf_ref, out_ref, *, final):
    h = h_ref[...]
    xn = _rms_rows(h, g_ref[...]).astype(BF16)
    gate = jnp.dot(xn, wpg_ref[...], preferred_element_type=F32)
    gate = 1.0 / (1.0 + jnp.exp(-gate))
    proj = jnp.dot(p_ref[...].astype(BF16), wpp_ref[...], preferred_element_type=F32)
    hn = h + gate * proj
    if final:
        hn = _rms_rows(hn, gf_ref[...])
    out_ref[...] = hn


def _ple(h, g, p, wpg, wpp, g_final, layer, tm, final):
    m = h.shape[0]
    row = lambda i: (i, 0)
    fixed = lambda i: (0, 0)
    return pl.pallas_call(
        functools.partial(_ple_kernel, final=final),
        grid=(m // tm,),
        in_specs=[
            pl.BlockSpec((tm, D_MODEL), row),
            pl.BlockSpec((1, D_MODEL), fixed),
            pl.BlockSpec((tm, PLE_DIM), row),
            pl.BlockSpec((None, D_MODEL, D_MODEL), lambda i: (layer, 0, 0)),
            pl.BlockSpec((None, PLE_DIM, D_MODEL), lambda i: (layer, 0, 0)),
            pl.BlockSpec((1, D_MODEL), fixed),
        ],
        out_specs=pl.BlockSpec((tm, D_MODEL), row),
        out_shape=jax.ShapeDtypeStruct((m, D_MODEL), F32),
        compiler_params=_cparams("arbitrary"),
        name="ple",
    )(h, g, p, wpg, wpp, g_final)


def _expand_state(state, seq_len):
    b, _, c = state.shape
    zeros = jnp.zeros((b, seq_len - 1, c), state.dtype)
    p1 = jnp.concatenate([state[:, 1:2], zeros], axis=1).reshape(b * seq_len, c)
    p2 = jnp.concatenate([state[:, 0:1], state[:, 1:2], zeros[:, 1:]], axis=1).reshape(b * seq_len, c)
    return p1, p2


def _pad_ff(x):
    return jnp.pad(x, ((0, 0), (0, D_FF_PAD - D_FF)))


def kernel(x_prompt, x_sample, cache_k, cache_v, state_conv_mix, state_conv_ffn, page_table,
           p_prompt, p_sample, g_mix, w_in, b_sb, conv_mix_w, g_v, w_s, b_s, g_out, w_o,
           g_ffn, w_up, conv_ffn_w, w_down, g_ple, w_ple_gate, w_ple_proj, g_final):
    depth = w_in.shape[0]
    bp, seq, _ = x_prompt.shape
    n_seq, n_q, _ = x_sample.shape
    assert bp == 1 and n_q >= 2 and n_q * n_seq == CHUNK
    ms = n_seq * n_q
    hp = x_prompt.reshape(seq, D_MODEL)
    hs = x_sample.reshape(ms, D_MODEL)
    gfin = g_final.reshape(1, D_MODEL)
    tril = jnp.tril(jnp.ones((CHUNK, CHUNK), F32))

    ff_pad = ((0, 0), (0, 0), (0, D_FF_PAD - D_FF))
    w_in_b = w_in.astype(BF16)
    w_o_b = w_o.astype(BF16)
    wg_b = jnp.pad(w_up[:, :, :D_FF], ff_pad).astype(BF16)
    wv_b = jnp.pad(w_up[:, :, D_FF:], ff_pad).astype(BF16)
    wd_b = jnp.pad(w_down, ((0, 0), (0, D_FF_PAD - D_FF), (0, 0))).astype(BF16)
    wpg_b = w_ple_gate.astype(BF16)
    wpp_b = w_ple_proj.astype(BF16)

    outs = {k: [] for k in ("kp", "vp", "cbp", "cfp", "ks", "vs", "cbs", "cfs", "cvs")}
    for l in range(depth):
        cwg = _pad_ff(conv_ffn_w[l][:, :D_FF])
        cwv = _pad_ff(conv_ffn_w[l][:, D_FF:])
        gm = g_mix[l].reshape(1, D_MODEL)
        gv = g_v[l].reshape(1, W_C)
        go = g_out[l].reshape(1, D_MODEL)
        gf = g_ffn[l].reshape(1, D_MODEL)
        gp = g_ple[l].reshape(1, D_MODEL)
        final = l == depth - 1

        ws_p = (w_s[l] * tril).astype(BF16)
        bs_p = jnp.repeat(b_s[l].T, HEAD_DIM, axis=1)
        wm_q = (w_s[l] * tril)[:, :n_q, :n_q]
        ws_s = jnp.einsum("ab,hts->hatbs", jnp.eye(n_seq, dtype=F32), wm_q).reshape(H_C, ms, ms).astype(BF16)
        bs_s = jnp.repeat(jnp.tile(b_s[l][:, :n_q].T, (n_seq, 1)), HEAD_DIM, axis=1)

        qkv, kp, vp, rest = _in_proj(hp, gm, w_in_b, l, 1024, head_rows=True)
        ya = _attention_prompt(qkv, b_sb[l])
        hp, xcp = _mix(hp, ya, rest, conv_mix_w[l], gv, ws_p, bs_p, go, w_o_b, l, 256)
        hp, cfg, cfv = _ffn(hp, gf, wg_b, wv_b, cwg, cwv, wd_b, l, 512)
        hp = _ple(hp, gp, p_prompt[l].reshape(seq, PLE_DIM), wpg_b, wpp_b, gfin, l, 512, final)
        outs["kp"].append(kp.reshape(1, seq, H_A, HEAD_DIM))
        outs["vp"].append(vp.reshape(1, seq, H_A, HEAD_DIM))
        outs["cbp"].append(xcp[seq - 2:].reshape(1, 2, W_B))
        outs["cfp"].append(jnp.concatenate([cfg[-2:, :D_FF], cfv[-2:, :D_FF]],
                                           axis=-1).reshape(1, 2, 2 * D_FF))

        qkv_s, ks, vs, rest_s = _in_proj(hs, gm, w_in_b, l, ms, head_rows=False)
        ya_s = _attention_sample(qkv_s, ks, vs, b_sb[l], cache_k, cache_v, page_table, l)
        p1m, p2m = _expand_state(state_conv_mix[l], n_q)
        hs, xcs, vvs = _mix(hs, ya_s, rest_s, conv_mix_w[l], gv, ws_s, bs_s, go, w_o_b, l, ms,
                            p1=p1m, p2=p2m, seq_len=n_q)
        p1g, p2g = _expand_state(_pad_ff(state_conv_ffn[l][..., :D_FF].reshape(n_seq * 2, D_FF))
                                 .reshape(n_seq, 2, D_FF_PAD), n_q)
        p1v, p2v = _expand_state(_pad_ff(state_conv_ffn[l][..., D_FF:].reshape(n_seq * 2, D_FF))
                                 .reshape(n_seq, 2, D_FF_PAD), n_q)
        hs, upg, upv = _ffn(hs, gf, wg_b, wv_b, cwg, cwv, wd_b, l, ms,
                            prevs=(p1g, p2g, p1v, p2v), seq_len=n_q)
        hs = _ple(hs, gp, p_sample[l].reshape(ms, PLE_DIM), wpg_b, wpp_b, gfin, l, ms, final)
        outs["ks"].append(ks.reshape(n_seq, n_q, H_A, HEAD_DIM))
        outs["vs"].append(vs.reshape(n_seq, n_q, H_A, HEAD_DIM))
        outs["cbs"].append(xcs.reshape(n_seq, n_q, W_B)[:, n_q - 2:])
        up_s = jnp.concatenate([upg[:, :D_FF], upv[:, :D_FF]], axis=-1)
        outs["cfs"].append(up_s.reshape(n_seq, n_q, 2 * D_FF)[:, n_q - 2:])
        outs["cvs"].append(vvs.reshape(n_seq, n_q, W_C))

    st = lambda k: jnp.stack(outs[k])
    return (hp.reshape(1, seq, D_MODEL), hs.reshape(n_seq, n_q, D_MODEL),
            st("kp"), st("vp"), st("cbp"), st("cfp"),
            st("ks"), st("vs"), st("cbs"), st("cfs"), st("cvs"))
```

```python
import functools

import numpy as np
import jax
import jax.numpy as jnp
from jax import lax
from jax.experimental import pallas as pl
from jax.experimental.pallas import tpu as pltpu

F32 = jnp.float32
BF16 = jnp.bfloat16

D_MODEL = 2048
HEAD_DIM = 128
W_A = 1024
W_B = 512
W_C = 512
H_A = W_A // HEAD_DIM
H_C = W_C // HEAD_DIM
N_HEADS = D_MODEL // HEAD_DIM
IN_W = 3 * W_A + 3 * W_B + 2 * W_C
REST_W = IN_W - 3 * W_A
D_FF = 5504
CHUNK = 128
PAGE = 128
PLE_DIM = 256
EPS = 1e-6
SCALE = HEAD_DIM ** -0.5
LOG2E = float(np.log2(np.e))
Q_PRESCALE = SCALE * LOG2E

LANE = 128
SUBLANE = 8
FF_TILE = 512
D_FF_PAD = ((D_FF + FF_TILE - 1) // FF_TILE) * FF_TILE
IN_TILE = 512
VMEM_LIMIT = 56 * 1024 * 1024


def _cparams(*sem):
    return pltpu.CompilerParams(dimension_semantics=sem, vmem_limit_bytes=VMEM_LIMIT)


def _rms_rows(x, g):
    ms = jnp.mean(x * x, axis=-1, keepdims=True)
    return x * lax.rsqrt(ms + EPS) * g


def _gelu_tanh(x):
    c = np.float32(np.sqrt(2.0 / np.pi))
    return 0.5 * x * (1.0 + jnp.tanh(c * (x + 0.044715 * (x * x * x))))


def _softplus2(t):
    return jnp.maximum(t, 0.0) + jnp.log2(1.0 + jnp.exp2(-jnp.abs(t)))


def _split_bf16(x):
    hi = x.astype(BF16)
    lo = (x - hi.astype(F32)).astype(BF16)
    return hi, lo


def _in_proj_kernel(x_ref, g_ref, w_ref, qkv_ref, k_ref, v_ref, rest_ref, xn_ref, *, head_rows):
    j = pl.program_id(1)
    tm = x_ref.shape[0]

    @pl.when(j == 0)
    def _():
        xn_ref[...] = _rms_rows(x_ref[...], g_ref[...]).astype(BF16)

    r = jnp.dot(xn_ref[...], w_ref[...], preferred_element_type=F32)
    nq = W_A // IN_TILE
    heads_per_tile = IN_TILE // HEAD_DIM

    @pl.when(j < nq)
    def _():
        qkv_ref[...] = (r * Q_PRESCALE).astype(BF16)

    @pl.when(jnp.logical_and(j >= nq, j < 3 * nq))
    def _():
        qkv_ref[...] = r.astype(BF16)

    def put_f32(out_ref, first_tile):
        if not head_rows:
            @pl.when(jnp.logical_and(j >= first_tile, j < first_tile + nq))
            def _():
                out_ref[...] = r
            return
        for jj in range(nq):
            @pl.when(j == first_tile + jj)
            def _(jj=jj):
                for hh in range(heads_per_tile):
                    out_ref[pl.ds(jj * heads_per_tile + hh, tm, stride=H_A), :] = (
                        r[:, hh * HEAD_DIM:(hh + 1) * HEAD_DIM])

    put_f32(k_ref, nq)
    put_f32(v_ref, 2 * nq)

    @pl.when(j >= 3 * nq)
    def _():
        rest_ref[...] = r


def _in_proj(x, g, w, layer, tm, head_rows):
    m = x.shape[0]
    nq = W_A // IN_TILE
    nj = IN_W // IN_TILE
    nrest = REST_W // IN_TILE
    if head_rows:
        kv_spec = lambda first: pl.BlockSpec((tm * H_A, HEAD_DIM), lambda i, j: (i, 0))
        kv_shape = jax.ShapeDtypeStruct((m * H_A, HEAD_DIM), F32)
    else:
        kv_spec = lambda first: pl.BlockSpec((tm, IN_TILE), lambda i, j: (i, jnp.clip(j - first, 0, nq - 1)))
        kv_shape = jax.ShapeDtypeStruct((m, W_A), F32)
    return pl.pallas_call(
        functools.partial(_in_proj_kernel, head_rows=head_rows),
        grid=(m // tm, nj),
        in_specs=[
            pl.BlockSpec((tm, D_MODEL), lambda i, j: (i, 0)),
            pl.BlockSpec((1, D_MODEL), lambda i, j: (0, 0)),
            pl.BlockSpec((None, D_MODEL, IN_TILE), lambda i, j: (layer, 0, j)),
        ],
        out_specs=[
            pl.BlockSpec((tm, IN_TILE), lambda i, j: (i, jnp.minimum(j, 3 * nq - 1))),
            kv_spec(nq),
            kv_spec(2 * nq),
            pl.BlockSpec((tm, IN_TILE), lambda i, j: (i, jnp.clip(j - 3 * nq, 0, nrest - 1))),
        ],
        out_shape=[
            jax.ShapeDtypeStruct((m, 3 * W_A), BF16),
            kv_shape,
            kv_shape,
            jax.ShapeDtypeStruct((m, REST_W), F32),
        ],
        scratch_shapes=[pltpu.VMEM((tm, D_MODEL), BF16)],
        compiler_params=_cparams("arbitrary", "arbitrary"),
        name="in_proj",
    )(x, g, w)


ATT_TQ = 256
ATT_TK = 128
ATT_SPAN = 512
ATT_HEADS = 4
ATT_UNIT = 2 * ATT_TK
ATT_QK_SKEW = 2
ATT_SKEW = 2
assert ATT_SPAN % ATT_TQ == 0 and H_A % ATT_HEADS == 0


def _cumsum_matrix():
    j = np.arange(ATT_UNIT)[:, None]
    s = np.arange(ATT_UNIT)[None, :]
    return jnp.asarray((j > s).astype(np.float32), dtype=BF16)


SCOLS = LANE
SATT_PAGES = 8


def _sample_unit_schedule(n_q_tiles, groups, n_units):
    per_pass = n_units // groups
    counts = np.array([((i * ATT_TQ) // ATT_SPAN) // 2 for i in range(n_q_tiles)], np.int64)
    rest = per_pass - int(counts.sum())
    assert per_pass * groups == n_units and rest >= 0
    k = 0
    while rest > 0:
        counts[n_q_tiles - 1 - (k % n_q_tiles)] += 1
        rest -= 1
        k += 1
    counts = np.tile(counts, groups)
    base = np.concatenate([[0], np.cumsum(counts)[:-1]])
    return jnp.asarray(base, jnp.int32), jnp.asarray(counts, jnp.int32)


def _fused_attn_kernel(pt_ref, ubase_ref, ucount_ref, b_ref, q_ref, k_ref, v_ref, u_ref,
                       qbd_ref, knew_ref, vnew_ref, ck_ref, cv_ref, brow_ref, nmask_ref, l_ref, hm_ref,
                       o_ref, os_ref,
                       acc_ref, carry_ref, sacc_ref, scarry_ref, new_ref, kbuf_ref, vbuf_ref, sem_ref,
                       *, layer, n_q, n_pages, n_units):
    i = pl.program_id(1)
    step = pl.program_id(0) * pl.num_programs(1) + i
    unit_base = ubase_ref[step]
    unit_count = ucount_ref[step]

    units_per_seq = n_pages // SATT_PAGES
    nrow = n_q * H_A
    brow = brow_ref[...]
    lmat = l_ref[...]

    def page_copies(u, slot):
        b = u // units_per_seq
        first = n_pages - (u % units_per_seq + 1) * SATT_PAGES
        copies = []
        for g in range(SATT_PAGES):
            page = pt_ref[b, first + g]
            copies.append(pltpu.make_async_copy(ck_ref.at[layer, page], kbuf_ref.at[slot, g], sem_ref.at[0, slot]))
            copies.append(pltpu.make_async_copy(cv_ref.at[layer, page], vbuf_ref.at[slot, g], sem_ref.at[1, slot]))
        return copies

    @pl.when(step == 0)
    def _():
        for c in page_copies(0, 0):
            c.start()

    def sample_unit(u):
        slot = u % 2
        b = u // units_per_seq
        p = u % units_per_seq

        @pl.when(u + 1 < n_units)
        def _():
            for c in page_copies(u + 1, 1 - slot):
                c.start()

        for c in page_copies(u, slot):
            c.wait()

        qbd = qbd_ref[b]

        def heads_to_lanes(buf_ref, g):
            return jnp.concatenate(
                [buf_ref[slot, g, pl.ds(hh, PAGE, stride=H_A), :].astype(BF16) for hh in range(H_A)], axis=1)

        def later_keys(sp):
            hi, lo = _split_bf16(sp)
            return jnp.dot(lmat, hi, preferred_element_type=F32) + jnp.dot(lmat, lo, preferred_element_type=F32)

        @pl.when(p == 0)
        def _():
            mask = nmask_ref[...]
            new_ref[...] = jnp.zeros_like(new_ref)
            new_ref[0, 0:n_q, :] = knew_ref[b]
            new_ref[1, 0:n_q, :] = vnew_ref[b]
            zt = jnp.dot(new_ref[0].astype(BF16), qbd, preferred_element_type=F32) + brow
            sp = jnp.where(mask > 0.0, _softplus2(zt), 0.0)
            a = jnp.where(mask > 0.0, jnp.exp2(zt - sp - later_keys(sp)), 0.0)
            sacc_ref[...] = jnp.dot(a.T[:nrow].astype(BF16), new_ref[1].astype(BF16), preferred_element_type=F32)
            scarry_ref[...] = jnp.sum(sp, axis=0, keepdims=True)

        pages = range(SATT_PAGES)
        zts = [jnp.dot(heads_to_lanes(kbuf_ref, g), qbd, preferred_element_type=F32) + brow for g in pages]
        sps = [_softplus2(zt) for zt in zts]
        laters = [later_keys(sp) for sp in sps]
        tots = [jnp.sum(sp, axis=0, keepdims=True) for sp in sps]
        cs = [None] * SATT_PAGES
        c = scarry_ref[...]
        for g in range(SATT_PAGES - 1, -1, -1):
            cs[g] = c
            c = c + tots[g]
        scarry_ref[...] = c
        ats = [jnp.exp2(zts[g] - sps[g] - laters[g] - cs[g]).T[:nrow].astype(BF16) for g in pages]
        outs = [jnp.dot(ats[g], heads_to_lanes(vbuf_ref, g), preferred_element_type=F32) for g in pages]
        sacc_ref[...] += sum(outs[1:], outs[0])

        @pl.when(p == units_per_seq - 1)
        def _():
            res = sacc_ref[...] * hm_ref[...]
            os_ref[b] = jnp.sum(res.reshape(n_q, H_A, W_A), axis=1)

    acc_ref[...] = jnp.zeros_like(acc_ref)
    carry_ref[...] = jnp.zeros_like(carry_ref)
    heads = range(ATT_HEADS)
    lanes = [slice(e * HEAD_DIM, (e + 1) * HEAD_DIM) for e in heads]
    bias2 = [b_ref[pl.program_id(0) * ATT_HEADS + e] * LOG2E for e in heads]

    def span(start, n_keys, masked):
        start = pl.multiple_of(start, ATT_SPAN)
        units = [(e, u) for u in range(n_keys // ATT_UNIT - 1, -1, -1) for e in heads]
        cs = [carry_ref[e] for e in heads]
        accs = [None for _ in heads]
        logits = {}
        pending = {}

        def stage_qk(e, u):
            k = k_ref[pl.ds(start + u * ATT_UNIT, ATT_UNIT), lanes[e]]
            logits[(e, u)] = lax.dot_general(q_ref[:, lanes[e]], k, (((1,), (1,)), ((), ())),
                                             preferred_element_type=F32) + bias2[e]

        def stage_scores(e, u):
            t = logits.pop((e, u))
            sp = _softplus2(t)
            valid = None
            if masked:
                row = lax.broadcasted_iota(jnp.int32, (ATT_TQ, ATT_UNIT), 0) + i * ATT_TQ
                col = lax.broadcasted_iota(jnp.int32, (ATT_TQ, ATT_UNIT), 1) + (start + u * ATT_UNIT)
                valid = col < row
                sp = jnp.where(valid, sp, 0.0)
            later = jnp.dot(sp.astype(BF16), u_ref[...], preferred_element_type=F32)
            total = jnp.broadcast_to(jnp.sum(sp, axis=1, keepdims=True), (ATT_TQ, ATT_TK))
            pending[(e, u)] = (t - sp - later, total, valid)

        def stage_out(e, u):
            x, total, valid = pending.pop((e, u))
            a = jnp.exp2(x - jnp.concatenate([cs[e]] * (ATT_UNIT // ATT_TK), axis=1))
            if masked:
                a = jnp.where(valid, a, 0.0)
            cs[e] = cs[e] + total
            v = v_ref[pl.ds(start + u * ATT_UNIT, ATT_UNIT), lanes[e]]
            o = jnp.dot(a.astype(BF16), v, preferred_element_type=F32)
            accs[e] = o if accs[e] is None else accs[e] + o

        n_work = len(units)
        for idx in range(n_work + ATT_QK_SKEW + ATT_SKEW):
            if idx < n_work:
                stage_qk(*units[idx])
            if 0 <= idx - ATT_QK_SKEW < n_work:
                stage_scores(*units[idx - ATT_QK_SKEW])
            if 0 <= idx - ATT_QK_SKEW - ATT_SKEW < n_work:
                stage_out(*units[idx - ATT_QK_SKEW - ATT_SKEW])
        for e in heads:
            carry_ref[e] = cs[e]
            acc_ref[e] += accs[e]

    n_full = (i * ATT_TQ) // ATT_SPAN
    span(n_full * ATT_SPAN, ATT_SPAN, True)

    @pl.when(n_full % 2 == 1)
    def _():
        span((n_full - 1) * ATT_SPAN, ATT_SPAN, False)

    n_pairs = n_full // 2

    def body(it, c):
        @pl.when(it < unit_count)
        def _():
            sample_unit(unit_base + it)

        span((n_pairs - 1 - it) * (2 * ATT_SPAN), 2 * ATT_SPAN, False)
        return c

    lax.fori_loop(0, n_pairs, body, 0)

    def leftover(n, c):
        sample_unit(unit_base + n)
        return c

    lax.fori_loop(jnp.minimum(n_pairs, unit_count), unit_count, leftover, 0)
    for e in heads:
        o_ref[:, lanes[e]] = acc_ref[e]


def _attention_fused(qkv, b_sb, qkv_s, k_new, v_new, cache_k, cache_v, page_table, layer):
    s = qkv.shape[0]
    width = ATT_HEADS * HEAD_DIM
    groups = H_A // ATT_HEADS
    n_q_tiles = s // ATT_TQ
    n_seq, n_pages = page_table.shape
    n_q = qkv_s.shape[0] // n_seq
    nrow = n_q * H_A
    assert n_pages % SATT_PAGES == 0
    n_units = n_seq * (n_pages // SATT_PAGES)
    ubase, ucount = _sample_unit_schedule(n_q_tiles, groups, n_units)

    q = qkv_s[:, :W_A].reshape(n_seq, n_q, H_A, HEAD_DIM)
    eye = jnp.eye(H_A, dtype=BF16)
    qbd = jnp.einsum("bthd,hg->bhdtg", q, eye).reshape(n_seq, W_A, nrow)
    qbd = jnp.pad(qbd, ((0, 0), (0, 0), (0, SCOLS - nrow)))
    knew = k_new.reshape(n_seq, n_q, W_A)
    vnew = v_new.reshape(n_seq, n_q, W_A)
    cols = np.arange(SCOLS)
    brow = jnp.where(cols < nrow, b_sb[cols % H_A] * LOG2E, 0.0).reshape(1, SCOLS).astype(F32)
    keys = np.arange(PAGE)[:, None]
    nmask = jnp.asarray(((cols[None, :] < nrow) & (keys < cols[None, :] // H_A)).astype(np.float32))
    lmat = jnp.asarray((np.arange(PAGE)[None, :] > keys).astype(np.float32), dtype=BF16)
    hm = jnp.asarray((np.arange(W_A)[None, :] // HEAD_DIM
                      == np.arange(nrow)[:, None] % H_A).astype(np.float32))
    n_pool = cache_k.shape[1]
    ck = cache_k.reshape(cache_k.shape[0], n_pool, PAGE * H_A, HEAD_DIM)
    cv = cache_v.reshape(cache_v.shape[0], n_pool, PAGE * H_A, HEAD_DIM)

    once = pl.Buffered(1)
    fixed2 = lambda h, i, *_: (0, 0)
    fixed3 = lambda h, i, *_: (0, 0, 0)
    grid_spec = pltpu.PrefetchScalarGridSpec(
        num_scalar_prefetch=3,
        grid=(groups, n_q_tiles),
        in_specs=[
            pl.BlockSpec(memory_space=pltpu.SMEM),
            pl.BlockSpec((ATT_TQ, width), lambda h, i, *_: (i, h)),
            pl.BlockSpec((s, width), lambda h, i, *_: (0, groups + h), pipeline_mode=once),
            pl.BlockSpec((s, width), lambda h, i, *_: (0, 2 * groups + h), pipeline_mode=once),
            pl.BlockSpec((ATT_UNIT, ATT_UNIT), fixed2),
            pl.BlockSpec((n_seq, W_A, SCOLS), fixed3, pipeline_mode=once),
            pl.BlockSpec((n_seq, n_q, W_A), fixed3),
            pl.BlockSpec((n_seq, n_q, W_A), fixed3),
            pl.BlockSpec(memory_space=pl.ANY),
            pl.BlockSpec(memory_space=pl.ANY),
            pl.BlockSpec((1, SCOLS), fixed2),
            pl.BlockSpec((PAGE, SCOLS), fixed2),
            pl.BlockSpec((PAGE, PAGE), fixed2),
            pl.BlockSpec((nrow, W_A), fixed2),
        ],
        out_specs=[
            pl.BlockSpec((ATT_TQ, width), lambda h, i, *_: (i, h)),
            pl.BlockSpec((n_seq, n_q, W_A), fixed3),
        ],
        scratch_shapes=[
            pltpu.VMEM((ATT_HEADS, ATT_TQ, HEAD_DIM), F32),
            pltpu.VMEM((ATT_HEADS, ATT_TQ, ATT_TK), F32),
            pltpu.VMEM((nrow, W_A), F32),
            pltpu.VMEM((1, SCOLS), F32),
            pltpu.VMEM((2, PAGE, W_A), F32),
            pltpu.VMEM((2, SATT_PAGES, PAGE * H_A, HEAD_DIM), F32),
            pltpu.VMEM((2, SATT_PAGES, PAGE * H_A, HEAD_DIM), F32),
            pltpu.SemaphoreType.DMA((2, 2)),
        ],
    )
    ya, ya_s = pl.pallas_call(
        functools.partial(_fused_attn_kernel, layer=layer, n_q=n_q, n_pages=n_pages, n_units=n_units),
        grid_spec=grid_spec,
        out_shape=[jax.ShapeDtypeStruct((s, W_A), F32), jax.ShapeDtypeStruct((n_seq, n_q, W_A), F32)],
        compiler_params=_cparams("arbitrary", "arbitrary"),
        name="attn_fused",
    )(page_table, ubase, ucount, b_sb, qkv, qkv, qkv, _cumsum_matrix(), qbd, knew, vnew, ck, cv,
      brow, nmask, lmat, hm)
    return ya, ya_s.reshape(n_seq * n_q, W_A)


def _conv3_rows(x, cw_ref, buf_ref, prev8, p1, p2, seq_len):
    tm = x.shape[0]
    buf_ref[pl.ds(0, SUBLANE), :] = prev8
    buf_ref[pl.ds(SUBLANE, tm), :] = x
    x1 = buf_ref[pl.ds(SUBLANE - 1, tm), :]
    x2 = buf_ref[pl.ds(SUBLANE - 2, tm), :]
    if p1 is not None:
        t = lax.broadcasted_iota(jnp.int32, x.shape, 0) % seq_len
        x1 = jnp.where(t == 0, p1, x1)
        x2 = jnp.where(t < 2, p2, x2)
    cw = cw_ref[...]
    return cw[0:1, :] * x2 + cw[1:2, :] * x1 + cw[2:3, :] * x


def _mix_kernel(*refs, sample, seq_len):
    if sample:
        (h_ref, ya_ref, rest_ref, cw_ref, gv_ref, ws_ref, bs_ref, go_ref, wo_ref, p1_ref, p2_ref,
         out_ref, x_ref, vv_ref, buf_ref, y_ref) = refs
    else:
        (h_ref, ya_ref, rest_ref, cw_ref, gv_ref, ws_ref, bs_ref, go_ref, wo_ref,
         out_ref, x_ref, carry_ref, buf_ref, y_ref) = refs
    tm = h_ref.shape[0]
    i = pl.program_id(0)

    bg = rest_ref[:, 0:W_B]
    x = rest_ref[:, W_B:2 * W_B] * rest_ref[:, 2 * W_B:3 * W_B]
    x_ref[...] = x
    if sample:
        conv = _conv3_rows(x, cw_ref, buf_ref, jnp.zeros((SUBLANE, W_B), F32),
                           p1_ref[...], p2_ref[...], seq_len)
    else:
        @pl.when(i == 0)
        def _():
            carry_ref[...] = jnp.zeros_like(carry_ref)
        conv = _conv3_rows(x, cw_ref, buf_ref, carry_ref[...], None, None, seq_len)
        carry_ref[...] = x[tm - SUBLANE:, :]
    yb = bg * conv

    go = go_ref[...]

    def put_head(hidx, yh):
        ms = jnp.mean(yh * yh, axis=-1, keepdims=True)
        lo = hidx * HEAD_DIM
        y_ref[:, lo:lo + HEAD_DIM] = (yh * lax.rsqrt(ms + EPS) * go[:, lo:lo + HEAD_DIM]).astype(BF16)

    for hh in range(H_A):
        put_head(hh, ya_ref[:, hh * HEAD_DIM:(hh + 1) * HEAD_DIM])
    for hh in range(W_B // HEAD_DIM):
        put_head(H_A + hh, yb[:, hh * HEAD_DIM:(hh + 1) * HEAD_DIM])

    gv = gv_ref[...]
    bs = bs_ref[...]
    for hh in range(H_C):
        lo = hh * HEAD_DIM
        u = _gelu_tanh(rest_ref[:, 3 * W_B + lo:3 * W_B + lo + HEAD_DIM])
        gvc = _gelu_tanh(rest_ref[:, 3 * W_B + W_C + lo:3 * W_B + W_C + lo + HEAD_DIM])
        ms = jnp.mean(gvc * gvc, axis=-1, keepdims=True)
        vv = gvc * lax.rsqrt(ms + EPS) * gv[:, lo:lo + HEAD_DIM]
        if sample:
            vv_ref[:, lo:lo + HEAD_DIM] = vv
        vvb = vv.astype(BF16)
        w = ws_ref[hh]
        parts = []
        for c in range(tm // CHUNK):
            parts.append(jnp.dot(w, vvb[c * CHUNK:(c + 1) * CHUNK, :], preferred_element_type=F32)
                         + bs[:, lo:lo + HEAD_DIM])
        s = parts[0] if len(parts) == 1 else jnp.concatenate(parts, axis=0)
        put_head(H_A + W_B // HEAD_DIM + hh, u * s)

    out_ref[...] = h_ref[...] + jnp.dot(y_ref[...], wo_ref[...], preferred_element_type=F32)


def _mix(h, ya, rest, cw, g_v, ws, bs, g_out, w_o, layer, tm, p1=None, p2=None, seq_len=1):
    m = h.shape[0]
    sample = p1 is not None
    row = lambda i: (i, 0)
    fixed = lambda i: (0, 0)
    in_specs = [
        pl.BlockSpec((tm, D_MODEL), row),
        pl.BlockSpec((tm, W_A), row),
        pl.BlockSpec((tm, REST_W), row),
        pl.BlockSpec((3, W_B), fixed),
        pl.BlockSpec((1, W_C), fixed),
        pl.BlockSpec((H_C, CHUNK, CHUNK), lambda i: (0, 0, 0)),
        pl.BlockSpec((CHUNK, W_C), fixed),
        pl.BlockSpec((1, D_MODEL), fixed),
        pl.BlockSpec((None, D_MODEL, D_MODEL), lambda i: (layer, 0, 0)),
    ]
    args = [h, ya, rest, cw, g_v, ws, bs, g_out, w_o]
    out_specs = [pl.BlockSpec((tm, D_MODEL), row), pl.BlockSpec((tm, W_B), row)]
    out_shape = [jax.ShapeDtypeStruct((m, D_MODEL), F32), jax.ShapeDtypeStruct((m, W_B), F32)]
    scratch = []
    if sample:
        in_specs += [pl.BlockSpec((tm, W_B), row), pl.BlockSpec((tm, W_B), row)]
        args += [p1, p2]
        out_specs.append(pl.BlockSpec((tm, W_C), row))
        out_shape.append(jax.ShapeDtypeStruct((m, W_C), F32))
    else:
        scratch.append(pltpu.VMEM((SUBLANE, W_B), F32))
    scratch += [pltpu.VMEM((tm + SUBLANE, W_B), F32), pltpu.VMEM((tm, D_MODEL), BF16)]
    return pl.pallas_call(
        functools.partial(_mix_kernel, sample=sample, seq_len=seq_len),
        grid=(m // tm,),
        in_specs=in_specs,
        out_specs=out_specs,
        out_shape=out_shape,
        scratch_shapes=scratch,
        compiler_params=_cparams("arbitrary"),
        name="mix_sample" if sample else "mix_prompt",
    )(*args)


def _ffn_kernel(*refs, sample, seq_len):
    if sample:
        (h_ref, g_ref, wg_ref, wv_ref, cwg_ref, cwv_ref, wd_ref, p1g_ref, p2g_ref, p1v_ref, p2v_ref,
         out_ref, upg_ref, upv_ref, xn_ref, acc_ref, buf_ref) = refs
    else:
        (h_ref, g_ref, wg_ref, wv_ref, cwg_ref, cwv_ref, wd_ref,
         out_ref, upg_ref, upv_ref, xn_ref, acc_ref, buf_ref, carry_ref) = refs
    tm = h_ref.shape[0]
    i = pl.program_id(0)
    j = pl.program_id(1)

    @pl.when(j == 0)
    def _():
        xn_ref[...] = _rms_rows(h_ref[...], g_ref[...]).astype(BF16)
        acc_ref[...] = jnp.zeros_like(acc_ref)

    if not sample:
        @pl.when(jnp.logical_and(i == 0, j == 0))
        def _():
            carry_ref[...] = jnp.zeros_like(carry_ref)

    def branch(w_ref, cw_ref, idx, up_ref, p1_ref, p2_ref):
        up = jnp.dot(xn_ref[...], w_ref[...], preferred_element_type=F32)
        if sample:
            up_ref[...] = up
            return _conv3_rows(up, cw_ref, buf_ref, jnp.zeros((SUBLANE, FF_TILE), F32),
                               p1_ref[...], p2_ref[...], seq_len)
        conv = _conv3_rows(up, cw_ref, buf_ref, carry_ref[idx, j], None, None, seq_len)
        last = up[tm - SUBLANE:, :]
        carry_ref[idx, j] = last
        up_ref[...] = last
        return conv

    if sample:
        gc = branch(wg_ref, cwg_ref, 0, upg_ref, p1g_ref, p2g_ref)
        vc = branch(wv_ref, cwv_ref, 1, upv_ref, p1v_ref, p2v_ref)
    else:
        gc = branch(wg_ref, cwg_ref, 0, upg_ref, None, None)
        vc = branch(wv_ref, cwv_ref, 1, upv_ref, None, None)
    act = gc * (1.0 / (1.0 + jnp.exp(-gc))) * vc
    acc_ref[...] += jnp.dot(act.astype(BF16), wd_ref[...], preferred_element_type=F32)

    @pl.when(j == pl.num_programs(1) - 1)
    def _():
        out_ref[...] = h_ref[...] + acc_ref[...]


def _ffn(h, g, wg, wv, cwg, cwv, wd, layer, tm, prevs=None, seq_len=1):
    m = h.shape[0]
    sample = prevs is not None
    nj = D_FF_PAD // FF_TILE
    row = lambda i, j: (i, 0)
    col = lambda i, j: (0, j)
    in_specs = [
        pl.BlockSpec((tm, D_MODEL), row),
        pl.BlockSpec((1, D_MODEL), lambda i, j: (0, 0)),
        pl.BlockSpec((None, D_MODEL, FF_TILE), lambda i, j: (layer, 0, j)),
        pl.BlockSpec((None, D_MODEL, FF_TILE), lambda i, j: (layer, 0, j)),
        pl.BlockSpec((3, FF_TILE), col),
        pl.BlockSpec((3, FF_TILE), col),
        pl.BlockSpec((None, FF_TILE, D_MODEL), lambda i, j: (layer, j, 0)),
    ]
    args = [h, g, wg, wv, cwg, cwv, wd]
    up_rows = tm if sample else SUBLANE
    up_spec = pl.BlockSpec((up_rows, FF_TILE), lambda i, j: (i, j))
    up_shape = jax.ShapeDtypeStruct((m // tm * up_rows, D_FF_PAD), F32)
    out_specs = [pl.BlockSpec((tm, D_MODEL), row), up_spec, up_spec]
    out_shape = [jax.ShapeDtypeStruct((m, D_MODEL), F32), up_shape, up_shape]
    scratch = [pltpu.VMEM((tm, D_MODEL), BF16), pltpu.VMEM((tm, D_MODEL), F32),
               pltpu.VMEM((tm + SUBLANE, FF_TILE), F32)]
    if sample:
        in_specs += [pl.BlockSpec((tm, FF_TILE), col)] * 4
        args += list(prevs)
    else:
        scratch.append(pltpu.VMEM((2, nj, SUBLANE, FF_TILE), F32))
    return pl.pallas_call(
        functools.partial(_ffn_kernel, sample=sample, seq_len=seq_len),
        grid=(m // tm, nj),
        in_specs=in_specs,
        out_specs=out_specs,
        out_shape=out_shape,
        scratch_shapes=scratch,
        compiler_params=_cparams("arbitrary", "arbitrary"),
        name="ffn_sample" if sample else "ffn_prompt",
    )(*args)


def _ple_kernel(h_ref, g_ref, p_ref, wpg_ref, wpp_ref, gf_ref, out_ref, *, final):
    h = h_ref[...]
    xn = _rms_rows(h, g_ref[...]).astype(BF16)
    gate = jnp.dot(xn, wpg_ref[...], preferred_element_type=F32)
    gate = 1.0 / (1.0 + jnp.exp(-gate))
    proj = jnp.dot(p_ref[...].astype(BF16), wpp_ref[...], preferred_element_type=F32)
    hn = h + gate * proj
    if final:
        hn = _rms_rows(hn, gf_ref[...])
    out_ref[...] = hn


def _ple(h, g, p, wpg, wpp, g_final, layer, tm, final):
    m = h.shape[0]
    row = lambda i: (i, 0)
    fixed = lambda i: (0, 0)
    return pl.pallas_call(
        functools.partial(_ple_kernel, final=final),
        grid=(m // tm,),
        in_specs=[
            pl.BlockSpec((tm, D_MODEL), row),
            pl.BlockSpec((1, D_MODEL), fixed),
            pl.BlockSpec((tm, PLE_DIM), row),
            pl.BlockSpec((None, D_MODEL, D_MODEL), lambda i: (layer, 0, 0)),
            pl.BlockSpec((None, PLE_DIM, D_MODEL), lambda i: (layer, 0, 0)),
            pl.BlockSpec((1, D_MODEL), fixed),
        ],
        out_specs=pl.BlockSpec((tm, D_MODEL), row),
        out_shape=jax.ShapeDtypeStruct((m, D_MODEL), F32),
        compiler_params=_cparams("arbitrary"),
        name="ple",
    )(h, g, p, wpg, wpp, g_final)


def _expand_state(state, seq_len):
    b, _, c = state.shape
    zeros = jnp.zeros((b, seq_len - 1, c), state.dtype)
    p1 = jnp.concatenate([state[:, 1:2], zeros], axis=1).reshape(b * seq_len, c)
    p2 = jnp.concatenate([state[:, 0:1], state[:, 1:2], zeros[:, 1:]], axis=1).reshape(b * seq_len, c)
    return p1, p2


def _pad_ff(x):
    return jnp.pad(x, ((0, 0), (0, D_FF_PAD - D_FF)))


def kernel(x_prompt, x_sample, cache_k, cache_v, state_conv_mix, state_conv_ffn, page_table,
           p_prompt, p_sample, g_mix, w_in, b_sb, conv_mix_w, g_v, w_s, b_s, g_out, w_o,
           g_ffn, w_up, conv_ffn_w, w_down, g_ple, w_ple_gate, w_ple_proj, g_final):
    depth = w_in.shape[0]
    bp, seq, _ = x_prompt.shape
    n_seq, n_q, _ = x_sample.shape
    assert bp == 1 and n_q >= 2 and n_q * n_seq == CHUNK
    ms = n_seq * n_q
    hp = x_prompt.reshape(seq, D_MODEL)
    hs = x_sample.reshape(ms, D_MODEL)
    gfin = g_final.reshape(1, D_MODEL)
    tril = jnp.tril(jnp.ones((CHUNK, CHUNK), F32))

    ff_pad = ((0, 0), (0, 0), (0, D_FF_PAD - D_FF))
    w_in_b = w_in.astype(BF16)
    w_o_b = w_o.astype(BF16)
    wg_b = jnp.pad(w_up[:, :, :D_FF], ff_pad).astype(BF16)
    wv_b = jnp.pad(w_up[:, :, D_FF:], ff_pad).astype(BF16)
    wd_b = jnp.pad(w_down, ((0, 0), (0, D_FF_PAD - D_FF), (0, 0))).astype(BF16)
    wpg_b = w_ple_gate.astype(BF16)
    wpp_b = w_ple_proj.astype(BF16)

    outs = {k: [] for k in ("kp", "vp", "cbp", "cfp", "ks", "vs", "cbs", "cfs", "cvs")}
    for l in range(depth):
        cwg = _pad_ff(conv_ffn_w[l][:, :D_FF])
        cwv = _pad_ff(conv_ffn_w[l][:, D_FF:])
        gm = g_mix[l].reshape(1, D_MODEL)
        gv = g_v[l].reshape(1, W_C)
        go = g_out[l].reshape(1, D_MODEL)
        gf = g_ffn[l].reshape(1, D_MODEL)
        gp = g_ple[l].reshape(1, D_MODEL)
        final = l == depth - 1

        ws_p = (w_s[l] * tril).astype(BF16)
        bs_p = jnp.repeat(b_s[l].T, HEAD_DIM, axis=1)
        wm_q = (w_s[l] * tril)[:, :n_q, :n_q]
        ws_s = jnp.einsum("ab,hts->hatbs", jnp.eye(n_seq, dtype=F32), wm_q).reshape(H_C, ms, ms).astype(BF16)
        bs_s = jnp.repeat(jnp.tile(b_s[l][:, :n_q].T, (n_seq, 1)), HEAD_DIM, axis=1)

        qkv, kp, vp, rest = _in_proj(hp, gm, w_in_b, l, 1024, head_rows=True)
        qkv_s, ks, vs, rest_s = _in_proj(hs, gm, w_in_b, l, ms, head_rows=False)
        ya, ya_s = _attention_fused(qkv, b_sb[l], qkv_s, ks, vs, cache_k, cache_v, page_table, l)
        hp, xcp = _mix(hp, ya, rest, conv_mix_w[l], gv, ws_p, bs_p, go, w_o_b, l, 256)
        hp, cfg, cfv = _ffn(hp, gf, wg_b, wv_b, cwg, cwv, wd_b, l, 512)
        hp = _ple(hp, gp, p_prompt[l].reshape(seq, PLE_DIM), wpg_b, wpp_b, gfin, l, 512, final)
        outs["kp"].append(kp.reshape(1, seq, H_A, HEAD_DIM))
        outs["vp"].append(vp.reshape(1, seq, H_A, HEAD_DIM))
        outs["cbp"].append(xcp[seq - 2:].reshape(1, 2, W_B))
        outs["cfp"].append(jnp.concatenate([cfg[-2:, :D_FF], cfv[-2:, :D_FF]],
                                           axis=-1).reshape(1, 2, 2 * D_FF))

        p1m, p2m = _expand_state(state_conv_mix[l], n_q)
        hs, xcs, vvs = _mix(hs, ya_s, rest_s, conv_mix_w[l], gv, ws_s, bs_s, go, w_o_b, l, ms,
                            p1=p1m, p2=p2m, seq_len=n_q)
        p1g, p2g = _expand_state(_pad_ff(state_conv_ffn[l][..., :D_FF].reshape(n_seq * 2, D_FF))
                                 .reshape(n_seq, 2, D_FF_PAD), n_q)
        p1v, p2v = _expand_state(_pad_ff(state_conv_ffn[l][..., D_FF:].reshape(n_seq * 2, D_FF))
                                 .reshape(n_seq, 2, D_FF_PAD), n_q)
        hs, upg, upv = _ffn(hs, gf, wg_b, wv_b, cwg, cwv, wd_b, l, ms,
                            prevs=(p1g, p2g, p1v, p2v), seq_len=n_q)
        hs = _ple(hs, gp, p_sample[l].reshape(ms, PLE_DIM), wpg_b, wpp_b, gfin, l, ms, final)
        outs["ks"].append(ks.reshape(n_seq, n_q, H_A, HEAD_DIM))
        outs["vs"].append(vs.reshape(n_seq, n_q, H_A, HEAD_DIM))
        outs["cbs"].append(xcs.reshape(n_seq, n_q, W_B)[:, n_q - 2:])
        up_s = jnp.concatenate([upg[:, :D_FF], upv[:, :D_FF]], axis=-1)
        outs["cfs"].append(up_s.reshape(n_seq, n_q, 2 * D_FF)[:, n_q - 2:])
        outs["cvs"].append(vvs.reshape(n_seq, n_q, W_C))

    st = lambda k: jnp.stack(outs[k])
    return (hp.reshape(1, seq, D_MODEL), hs.reshape(n_seq, n_q, D_MODEL),
            st("kp"), st("vp"), st("cbp"), st("cfp"),
            st("ks"), st("vs"), st("cbs"), st("cfs"), st("cvs"))
```

```python
import functools

import numpy as np
import jax
import jax.numpy as jnp
from jax import lax
from jax.experimental import pallas as pl
from jax.experimental.pallas import tpu as pltpu

F32 = jnp.float32
BF16 = jnp.bfloat16

D_MODEL = 2048
HEAD_DIM = 128
W_A = 1024
W_B = 512
W_C = 512
H_A = W_A // HEAD_DIM
H_C = W_C // HEAD_DIM
N_HEADS = D_MODEL // HEAD_DIM
IN_W = 3 * W_A + 3 * W_B + 2 * W_C
REST_W = IN_W - 3 * W_A
D_FF = 5504
CHUNK = 128
PAGE = 128
PLE_DIM = 256
EPS = 1e-6
SCALE = HEAD_DIM ** -0.5
LOG2E = float(np.log2(np.e))
Q_PRESCALE = SCALE * LOG2E

LANE = 128
SUBLANE = 8
FF_TILE = 512
D_FF_PAD = ((D_FF + FF_TILE - 1) // FF_TILE) * FF_TILE
IN_TILE = 512
VMEM_LIMIT = 56 * 1024 * 1024


def _cparams(*sem):
    return pltpu.CompilerParams(dimension_semantics=sem, vmem_limit_bytes=VMEM_LIMIT)


def _rms_rows(x, g):
    ms = jnp.mean(x * x, axis=-1, keepdims=True)
    return x * lax.rsqrt(ms + EPS) * g


def _gelu_tanh(x):
    c = np.float32(np.sqrt(2.0 / np.pi))
    return 0.5 * x * (1.0 + jnp.tanh(c * (x + 0.044715 * (x * x * x))))


def _softplus2(t):
    return jnp.maximum(t, 0.0) + jnp.log2(1.0 + jnp.exp2(-jnp.abs(t)))


def _split_bf16(x):
    hi = x.astype(BF16)
    lo = (x - hi.astype(F32)).astype(BF16)
    return hi, lo


def _in_proj_kernel(x_ref, g_ref, w_ref, qkv_ref, k_ref, v_ref, rest_ref, xn_ref, *, head_rows):
    j = pl.program_id(1)
    tm = x_ref.shape[0]

    @pl.when(j == 0)
    def _():
        xn_ref[...] = _rms_rows(x_ref[...], g_ref[...]).astype(BF16)

    r = jnp.dot(xn_ref[...], w_ref[...], preferred_element_type=F32)
    nq = W_A // IN_TILE
    heads_per_tile = IN_TILE // HEAD_DIM

    @pl.when(j < nq)
    def _():
        qkv_ref[...] = (r * Q_PRESCALE).astype(BF16)

    @pl.when(jnp.logical_and(j >= nq, j < 3 * nq))
    def _():
        qkv_ref[...] = r.astype(BF16)

    def put_f32(out_ref, first_tile):
        if not head_rows:
            @pl.when(jnp.logical_and(j >= first_tile, j < first_tile + nq))
            def _():
                out_ref[...] = r
            return
        for jj in range(nq):
            @pl.when(j == first_tile + jj)
            def _(jj=jj):
                for hh in range(heads_per_tile):
                    out_ref[pl.ds(jj * heads_per_tile + hh, tm, stride=H_A), :] = (
                        r[:, hh * HEAD_DIM:(hh + 1) * HEAD_DIM])

    put_f32(k_ref, nq)
    put_f32(v_ref, 2 * nq)

    @pl.when(j >= 3 * nq)
    def _():
        rest_ref[...] = r


def _in_proj(x, g, w, layer, tm, head_rows):
    m = x.shape[0]
    nq = W_A // IN_TILE
    nj = IN_W // IN_TILE
    nrest = REST_W // IN_TILE
    if head_rows:
        kv_spec = lambda first: pl.BlockSpec((tm * H_A, HEAD_DIM), lambda i, j: (i, 0))
        kv_shape = jax.ShapeDtypeStruct((m * H_A, HEAD_DIM), F32)
    else:
        kv_spec = lambda first: pl.BlockSpec((tm, IN_TILE), lambda i, j: (i, jnp.clip(j - first, 0, nq - 1)))
        kv_shape = jax.ShapeDtypeStruct((m, W_A), F32)
    return pl.pallas_call(
        functools.partial(_in_proj_kernel, head_rows=head_rows),
        grid=(m // tm, nj),
        in_specs=[
            pl.BlockSpec((tm, D_MODEL), lambda i, j: (i, 0)),
            pl.BlockSpec((1, D_MODEL), lambda i, j: (0, 0)),
            pl.BlockSpec((None, D_MODEL, IN_TILE), lambda i, j: (layer, 0, j)),
        ],
        out_specs=[
            pl.BlockSpec((tm, IN_TILE), lambda i, j: (i, jnp.minimum(j, 3 * nq - 1))),
            kv_spec(nq),
            kv_spec(2 * nq),
            pl.BlockSpec((tm, IN_TILE), lambda i, j: (i, jnp.clip(j - 3 * nq, 0, nrest - 1))),
        ],
        out_shape=[
            jax.ShapeDtypeStruct((m, 3 * W_A), BF16),
            kv_shape,
            kv_shape,
            jax.ShapeDtypeStruct((m, REST_W), F32),
        ],
        scratch_shapes=[pltpu.VMEM((tm, D_MODEL), BF16)],
        compiler_params=_cparams("arbitrary", "arbitrary"),
        name="in_proj",
    )(x, g, w)


ATT_TQ = 256
ATT_TK = 128
ATT_SPAN = 512
ATT_HEADS = 4
ATT_UNIT = 2 * ATT_TK
ATT_QK_SKEW = 2
ATT_SKEW = 2
assert ATT_SPAN % ATT_TQ == 0 and H_A % ATT_HEADS == 0


def _cumsum_matrix():
    j = np.arange(ATT_UNIT)[:, None]
    s = np.arange(ATT_UNIT)[None, :]
    return jnp.asarray((j > s).astype(np.float32), dtype=BF16)


SCOLS = LANE
SATT_PAGES = 8


def _sample_unit_schedule(n_q_tiles, groups, n_units):
    per_pass = n_units // groups
    counts = np.array([((i * ATT_TQ) // ATT_SPAN) // 2 for i in range(n_q_tiles)], np.int64)
    rest = per_pass - int(counts.sum())
    assert per_pass * groups == n_units and rest >= 0
    k = 0
    while rest > 0:
        counts[n_q_tiles - 1 - (k % n_q_tiles)] += 1
        rest -= 1
        k += 1
    counts = np.tile(counts, groups)
    base = np.concatenate([[0], np.cumsum(counts)[:-1]])
    return jnp.asarray(base, jnp.int32), jnp.asarray(counts, jnp.int32)


def _fused_attn_kernel(pt_ref, ubase_ref, ucount_ref, b_ref, q_ref, k_ref, v_ref, u_ref,
                       qbd_ref, knew_ref, vnew_ref, ck_ref, cv_ref, brow_ref, nmask_ref, l_ref, hm_ref,
                       o_ref, os_ref,
                       acc_ref, carry_ref, sacc_ref, scarry_ref, new_ref, kbuf_ref, vbuf_ref, sem_ref,
                       *, layer, n_q, n_pages, n_units):
    i = pl.program_id(1)
    step = pl.program_id(0) * pl.num_programs(1) + i
    unit_base = ubase_ref[step]
    unit_count = ucount_ref[step]

    units_per_seq = n_pages // SATT_PAGES
    nrow = n_q * H_A
    brow = brow_ref[...]
    lmat = l_ref[...]

    def page_copies(u, slot):
        b = u // units_per_seq
        first = n_pages - (u % units_per_seq + 1) * SATT_PAGES
        copies = []
        for g in range(SATT_PAGES):
            page = pt_ref[b, first + g]
            copies.append(pltpu.make_async_copy(ck_ref.at[layer, page], kbuf_ref.at[slot, g], sem_ref.at[0, slot]))
            copies.append(pltpu.make_async_copy(cv_ref.at[layer, page], vbuf_ref.at[slot, g], sem_ref.at[1, slot]))
        return copies

    @pl.when(step == 0)
    def _():
        for c in page_copies(0, 0):
            c.start()

    def sample_unit(u):
        slot = u % 2
        b = u // units_per_seq
        p = u % units_per_seq

        @pl.when(u + 1 < n_units)
        def _():
            for c in page_copies(u + 1, 1 - slot):
                c.start()

        for c in page_copies(u, slot):
            c.wait()

        qbd = qbd_ref[b]

        def heads_to_lanes(buf_ref, g):
            return jnp.concatenate(
                [buf_ref[slot, g, pl.ds(hh, PAGE, stride=H_A), :].astype(BF16) for hh in range(H_A)], axis=1)

        def later_keys(sp):
            hi, lo = _split_bf16(sp)
            return jnp.dot(lmat, hi, preferred_element_type=F32) + jnp.dot(lmat, lo, preferred_element_type=F32)

        @pl.when(p == 0)
        def _():
            mask = nmask_ref[...]
            new_ref[...] = jnp.zeros_like(new_ref)
            new_ref[0, 0:n_q, :] = knew_ref[b]
            new_ref[1, 0:n_q, :] = vnew_ref[b]
            zt = jnp.dot(new_ref[0].astype(BF16), qbd, preferred_element_type=F32) + brow
            sp = jnp.where(mask > 0.0, _softplus2(zt), 0.0)
            a = jnp.where(mask > 0.0, jnp.exp2(zt - sp - later_keys(sp)), 0.0)
            sacc_ref[...] = jnp.dot(a.T[:nrow].astype(BF16), new_ref[1].astype(BF16), preferred_element_type=F32)
            scarry_ref[...] = jnp.sum(sp, axis=0, keepdims=True)

        pages = range(SATT_PAGES)
        zts = [jnp.dot(heads_to_lanes(kbuf_ref, g), qbd, preferred_element_type=F32) + brow for g in pages]
        sps = [_softplus2(zt) for zt in zts]
        laters = [later_keys(sp) for sp in sps]
        tots = [jnp.sum(sp, axis=0, keepdims=True) for sp in sps]
        cs = [None] * SATT_PAGES
        c = scarry_ref[...]
        for g in range(SATT_PAGES - 1, -1, -1):
            cs[g] = c
            c = c + tots[g]
        scarry_ref[...] = c
        ats = [jnp.exp2(zts[g] - sps[g] - laters[g] - cs[g]).T[:nrow].astype(BF16) for g in pages]
        outs = [jnp.dot(ats[g], heads_to_lanes(vbuf_ref, g), preferred_element_type=F32) for g in pages]
        sacc_ref[...] += sum(outs[1:], outs[0])

        @pl.when(p == units_per_seq - 1)
        def _():
            res = sacc_ref[...] * hm_ref[...]
            os_ref[b] = jnp.sum(res.reshape(n_q, H_A, W_A), axis=1)

    acc_ref[...] = jnp.zeros_like(acc_ref)
    carry_ref[...] = jnp.zeros_like(carry_ref)
    heads = range(ATT_HEADS)
    lanes = [slice(e * HEAD_DIM, (e + 1) * HEAD_DIM) for e in heads]
    bias2 = [b_ref[pl.program_id(0) * ATT_HEADS + e] * LOG2E for e in heads]

    def span(start, n_keys, masked):
        start = pl.multiple_of(start, ATT_SPAN)
        units = [(e, u) for u in range(n_keys // ATT_UNIT - 1, -1, -1) for e in heads]
        cs = [carry_ref[e] for e in heads]
        accs = [None for _ in heads]
        logits = {}
        pending = {}

        def stage_qk(e, u):
            k = k_ref[pl.ds(start + u * ATT_UNIT, ATT_UNIT), lanes[e]]
            logits[(e, u)] = lax.dot_general(q_ref[:, lanes[e]], k, (((1,), (1,)), ((), ())),
                                             preferred_element_type=F32) + bias2[e]

        def stage_scores(e, u):
            t = logits.pop((e, u))
            sp = _softplus2(t)
            valid = None
            if masked:
                row = lax.broadcasted_iota(jnp.int32, (ATT_TQ, ATT_UNIT), 0) + i * ATT_TQ
                col = lax.broadcasted_iota(jnp.int32, (ATT_TQ, ATT_UNIT), 1) + (start + u * ATT_UNIT)
                valid = col < row
                sp = jnp.where(valid, sp, 0.0)
            later = jnp.dot(sp.astype(BF16), u_ref[...], preferred_element_type=F32)
            total = jnp.broadcast_to(jnp.sum(sp, axis=1, keepdims=True), (ATT_TQ, ATT_TK))
            pending[(e, u)] = (t - sp - later, total, valid)

        def stage_out(e, u):
            x, total, valid = pending.pop((e, u))
            a = jnp.exp2(x - jnp.concatenate([cs[e]] * (ATT_UNIT // ATT_TK), axis=1))
            if masked:
                a = jnp.where(valid, a, 0.0)
            cs[e] = cs[e] + total
            v = v_ref[pl.ds(start + u * ATT_UNIT, ATT_UNIT), lanes[e]]
            o = jnp.dot(a.astype(BF16), v, preferred_element_type=F32)
            accs[e] = o if accs[e] is None else accs[e] + o

        n_work = len(units)
        for idx in range(n_work + ATT_QK_SKEW + ATT_SKEW):
            if idx < n_work:
                stage_qk(*units[idx])
            if 0 <= idx - ATT_QK_SKEW < n_work:
                stage_scores(*units[idx - ATT_QK_SKEW])
            if 0 <= idx - ATT_QK_SKEW - ATT_SKEW < n_work:
                stage_out(*units[idx - ATT_QK_SKEW - ATT_SKEW])
        for e in heads:
            carry_ref[e] = cs[e]
            acc_ref[e] += accs[e]

    n_full = (i * ATT_TQ) // ATT_SPAN
    span(n_full * ATT_SPAN, ATT_SPAN, True)

    @pl.when(n_full % 2 == 1)
    def _():
        span((n_full - 1) * ATT_SPAN, ATT_SPAN, False)

    n_pairs = n_full // 2

    def body(it, c):
        @pl.when(it < unit_count)
        def _():
            sample_unit(unit_base + it)

        span((n_pairs - 1 - it) * (2 * ATT_SPAN), 2 * ATT_SPAN, False)
        return c

    lax.fori_loop(0, n_pairs, body, 0)

    def leftover(n, c):
        sample_unit(unit_base + n)
        return c

    lax.fori_loop(jnp.minimum(n_pairs, unit_count), unit_count, leftover, 0)
    for e in heads:
        o_ref[:, lanes[e]] = acc_ref[e]


def _attention_fused(qkv, b_sb, qkv_s, k_new, v_new, cache_k, cache_v, page_table, layer):
    s = qkv.shape[0]
    width = ATT_HEADS * HEAD_DIM
    groups = H_A // ATT_HEADS
    n_q_tiles = s // ATT_TQ
    n_seq, n_pages = page_table.shape
    n_q = qkv_s.shape[0] // n_seq
    nrow = n_q * H_A
    assert n_pages % SATT_PAGES == 0
    n_units = n_seq * (n_pages // SATT_PAGES)
    ubase, ucount = _sample_unit_schedule(n_q_tiles, groups, n_units)

    q = qkv_s[:, :W_A].reshape(n_seq, n_q, H_A, HEAD_DIM)
    eye = jnp.eye(H_A, dtype=BF16)
    qbd = jnp.einsum("bthd,hg->bhdtg", q, eye).reshape(n_seq, W_A, nrow)
    qbd = jnp.pad(qbd, ((0, 0), (0, 0), (0, SCOLS - nrow)))
    knew = k_new.reshape(n_seq, n_q, W_A)
    vnew = v_new.reshape(n_seq, n_q, W_A)
    cols = np.arange(SCOLS)
    brow = jnp.where(cols < nrow, b_sb[cols % H_A] * LOG2E, 0.0).reshape(1, SCOLS).astype(F32)
    keys = np.arange(PAGE)[:, None]
    nmask = jnp.asarray(((cols[None, :] < nrow) & (keys < cols[None, :] // H_A)).astype(np.float32))
    lmat = jnp.asarray((np.arange(PAGE)[None, :] > keys).astype(np.float32), dtype=BF16)
    hm = jnp.asarray((np.arange(W_A)[None, :] // HEAD_DIM
                      == np.arange(nrow)[:, None] % H_A).astype(np.float32))
    n_pool = cache_k.shape[1]
    ck = cache_k.reshape(cache_k.shape[0], n_pool, PAGE * H_A, HEAD_DIM)
    cv = cache_v.reshape(cache_v.shape[0], n_pool, PAGE * H_A, HEAD_DIM)

    once = pl.Buffered(1)
    fixed2 = lambda h, i, *_: (0, 0)
    fixed3 = lambda h, i, *_: (0, 0, 0)
    grid_spec = pltpu.PrefetchScalarGridSpec(
        num_scalar_prefetch=3,
        grid=(groups, n_q_tiles),
        in_specs=[
            pl.BlockSpec(memory_space=pltpu.SMEM),
            pl.BlockSpec((ATT_TQ, width), lambda h, i, *_: (i, h)),
            pl.BlockSpec((s, width), lambda h, i, *_: (0, groups + h), pipeline_mode=once),
            pl.BlockSpec((s, width), lambda h, i, *_: (0, 2 * groups + h), pipeline_mode=once),
            pl.BlockSpec((ATT_UNIT, ATT_UNIT), fixed2),
            pl.BlockSpec((n_seq, W_A, SCOLS), fixed3, pipeline_mode=once),
            pl.BlockSpec((n_seq, n_q, W_A), fixed3),
            pl.BlockSpec((n_seq, n_q, W_A), fixed3),
            pl.BlockSpec(memory_space=pl.ANY),
            pl.BlockSpec(memory_space=pl.ANY),
            pl.BlockSpec((1, SCOLS), fixed2),
            pl.BlockSpec((PAGE, SCOLS), fixed2),
            pl.BlockSpec((PAGE, PAGE), fixed2),
            pl.BlockSpec((nrow, W_A), fixed2),
        ],
        out_specs=[
            pl.BlockSpec((ATT_TQ, width), lambda h, i, *_: (i, h)),
            pl.BlockSpec((n_seq, n_q, W_A), fixed3),
        ],
        scratch_shapes=[
            pltpu.VMEM((ATT_HEADS, ATT_TQ, HEAD_DIM), F32),
            pltpu.VMEM((ATT_HEADS, ATT_TQ, ATT_TK), F32),
            pltpu.VMEM((nrow, W_A), F32),
            pltpu.VMEM((1, SCOLS), F32),
            pltpu.VMEM((2, PAGE, W_A), F32),
            pltpu.VMEM((2, SATT_PAGES, PAGE * H_A, HEAD_DIM), F32),
            pltpu.VMEM((2, SATT_PAGES, PAGE * H_A, HEAD_DIM), F32),
            pltpu.SemaphoreType.DMA((2, 2)),
        ],
    )
    ya, ya_s = pl.pallas_call(
        functools.partial(_fused_attn_kernel, layer=layer, n_q=n_q, n_pages=n_pages, n_units=n_units),
        grid_spec=grid_spec,
        out_shape=[jax.ShapeDtypeStruct((s, W_A), F32), jax.ShapeDtypeStruct((n_seq, n_q, W_A), F32)],
        compiler_params=_cparams("arbitrary", "arbitrary"),
        name="attn_fused",
    )(page_table, ubase, ucount, b_sb, qkv, qkv, qkv, _cumsum_matrix(), qbd, knew, vnew, ck, cv,
      brow, nmask, lmat, hm)
    return ya, ya_s.reshape(n_seq * n_q, W_A)


def _conv3_rows(x, cw_ref, buf_ref, prev8, p1, p2, seq_len):
    tm = x.shape[0]
    buf_ref[pl.ds(0, SUBLANE), :] = prev8
    buf_ref[pl.ds(SUBLANE, tm), :] = x
    x1 = buf_ref[pl.ds(SUBLANE - 1, tm), :]
    x2 = buf_ref[pl.ds(SUBLANE - 2, tm), :]
    if p1 is not None:
        t = lax.broadcasted_iota(jnp.int32, x.shape, 0) % seq_len
        x1 = jnp.where(t == 0, p1, x1)
        x2 = jnp.where(t < 2, p2, x2)
    cw = cw_ref[...]
    return cw[0:1, :] * x2 + cw[1:2, :] * x1 + cw[2:3, :] * x


def _mix_kernel(*refs, sample, seq_len):
    if sample:
        (h_ref, ya_ref, rest_ref, cw_ref, gv_ref, ws_ref, bs_ref, go_ref, wo_ref, p1_ref, p2_ref,
         out_ref, x_ref, vv_ref, buf_ref, y_ref) = refs
    else:
        (h_ref, ya_ref, rest_ref, cw_ref, gv_ref, ws_ref, bs_ref, go_ref, wo_ref,
         out_ref, x_ref, carry_ref, buf_ref, y_ref) = refs
    tm = h_ref.shape[0]
    i = pl.program_id(0)

    bg = rest_ref[:, 0:W_B]
    x = rest_ref[:, W_B:2 * W_B] * rest_ref[:, 2 * W_B:3 * W_B]
    x_ref[...] = x
    if sample:
        conv = _conv3_rows(x, cw_ref, buf_ref, jnp.zeros((SUBLANE, W_B), F32),
                           p1_ref[...], p2_ref[...], seq_len)
    else:
        @pl.when(i == 0)
        def _():
            carry_ref[...] = jnp.zeros_like(carry_ref)
        conv = _conv3_rows(x, cw_ref, buf_ref, carry_ref[...], None, None, seq_len)
        carry_ref[...] = x[tm - SUBLANE:, :]
    yb = bg * conv

    go = go_ref[...]

    def put_head(hidx, yh):
        ms = jnp.mean(yh * yh, axis=-1, keepdims=True)
        lo = hidx * HEAD_DIM
        y_ref[:, lo:lo + HEAD_DIM] = (yh * lax.rsqrt(ms + EPS) * go[:, lo:lo + HEAD_DIM]).astype(BF16)

    for hh in range(H_A):
        put_head(hh, ya_ref[:, hh * HEAD_DIM:(hh + 1) * HEAD_DIM])
    for hh in range(W_B // HEAD_DIM):
        put_head(H_A + hh, yb[:, hh * HEAD_DIM:(hh + 1) * HEAD_DIM])

    gv = gv_ref[...]
    bs = bs_ref[...]
    for hh in range(H_C):
        lo = hh * HEAD_DIM
        u = _gelu_tanh(rest_ref[:, 3 * W_B + lo:3 * W_B + lo + HEAD_DIM])
        gvc = _gelu_tanh(rest_ref[:, 3 * W_B + W_C + lo:3 * W_B + W_C + lo + HEAD_DIM])
        ms = jnp.mean(gvc * gvc, axis=-1, keepdims=True)
        vv = gvc * lax.rsqrt(ms + EPS) * gv[:, lo:lo + HEAD_DIM]
        if sample:
            vv_ref[:, lo:lo + HEAD_DIM] = vv
        vvb = vv.astype(BF16)
        w = ws_ref[hh]
        parts = []
        for c in range(tm // CHUNK):
            parts.append(jnp.dot(w, vvb[c * CHUNK:(c + 1) * CHUNK, :], preferred_element_type=F32)
                         + bs[:, lo:lo + HEAD_DIM])
        s = parts[0] if len(parts) == 1 else jnp.concatenate(parts, axis=0)
        put_head(H_A + W_B // HEAD_DIM + hh, u * s)

    out_ref[...] = h_ref[...] + jnp.dot(y_ref[...], wo_ref[...], preferred_element_type=F32)


def _mix(h, ya, rest, cw, g_v, ws, bs, g_out, w_o, layer, tm, p1=None, p2=None, seq_len=1):
    m = h.shape[0]
    sample = p1 is not None
    row = lambda i: (i, 0)
    fixed = lambda i: (0, 0)
    in_specs = [
        pl.BlockSpec((tm, D_MODEL), row),
        pl.BlockSpec((tm, W_A), row),
        pl.BlockSpec((tm, REST_W), row),
        pl.BlockSpec((3, W_B), fixed),
        pl.BlockSpec((1, W_C), fixed),
        pl.BlockSpec((H_C, CHUNK, CHUNK), lambda i: (0, 0, 0)),
        pl.BlockSpec((CHUNK, W_C), fixed),
        pl.BlockSpec((1, D_MODEL), fixed),
        pl.BlockSpec((None, D_MODEL, D_MODEL), lambda i: (layer, 0, 0)),
    ]
    args = [h, ya, rest, cw, g_v, ws, bs, g_out, w_o]
    out_specs = [pl.BlockSpec((tm, D_MODEL), row), pl.BlockSpec((tm, W_B), row)]
    out_shape = [jax.ShapeDtypeStruct((m, D_MODEL), F32), jax.ShapeDtypeStruct((m, W_B), F32)]
    scratch = []
    if sample:
        in_specs += [pl.BlockSpec((tm, W_B), row), pl.BlockSpec((tm, W_B), row)]
        args += [p1, p2]
        out_specs.append(pl.BlockSpec((tm, W_C), row))
        out_shape.append(jax.ShapeDtypeStruct((m, W_C), F32))
    else:
        scratch.append(pltpu.VMEM((SUBLANE, W_B), F32))
    scratch += [pltpu.VMEM((tm + SUBLANE, W_B), F32), pltpu.VMEM((tm, D_MODEL), BF16)]
    return pl.pallas_call(
        functools.partial(_mix_kernel, sample=sample, seq_len=seq_len),
        grid=(m // tm,),
        in_specs=in_specs,
        out_specs=out_specs,
        out_shape=out_shape,
        scratch_shapes=scratch,
        compiler_params=_cparams("arbitrary"),
        name="mix_sample" if sample else "mix_prompt",
    )(*args)


def _ffn_kernel(*refs, sample, seq_len):
    if sample:
        (h_ref, g_ref, wg_ref, wv_ref, cwg_ref, cwv_ref, wd_ref, p1g_ref, p2g_ref, p1v_ref, p2v_ref,
         out_ref, upg_ref, upv_ref, xn_ref, buf_ref) = refs
    else:
        (h_ref, g_ref, wg_ref, wv_ref, cwg_ref, cwv_ref, wd_ref,
         out_ref, upg_ref, upv_ref, xn_ref, buf_ref, carry_ref) = refs
    tm = h_ref.shape[0]
    i = pl.program_id(0)
    j = pl.program_id(1)

    @pl.when(j == 0)
    def _():
        xn_ref[...] = _rms_rows(h_ref[...], g_ref[...]).astype(BF16)
        out_ref[...] = h_ref[...]

    if not sample:
        @pl.when(jnp.logical_and(i == 0, j == 0))
        def _():
            carry_ref[...] = jnp.zeros_like(carry_ref)

    def branch(w_ref, cw_ref, idx, up_ref, p1_ref, p2_ref):
        up = jnp.dot(xn_ref[...], w_ref[...], preferred_element_type=F32)
        if sample:
            up_ref[...] = up
            return _conv3_rows(up, cw_ref, buf_ref, jnp.zeros((SUBLANE, FF_TILE), F32),
                               p1_ref[...], p2_ref[...], seq_len)
        conv = _conv3_rows(up, cw_ref, buf_ref, carry_ref[idx, j], None, None, seq_len)
        last = up[tm - SUBLANE:, :]
        carry_ref[idx, j] = last
        up_ref[...] = last
        return conv

    if sample:
        gc = branch(wg_ref, cwg_ref, 0, upg_ref, p1g_ref, p2g_ref)
        vc = branch(wv_ref, cwv_ref, 1, upv_ref, p1v_ref, p2v_ref)
    else:
        gc = branch(wg_ref, cwg_ref, 0, upg_ref, None, None)
        vc = branch(wv_ref, cwv_ref, 1, upv_ref, None, None)
    act = gc * (1.0 / (1.0 + jnp.exp(-gc))) * vc
    out_ref[...] += jnp.dot(act.astype(BF16), wd_ref[...], preferred_element_type=F32)


def _ffn(h, g, wg, wv, cwg, cwv, wd, layer, tm, prevs=None, seq_len=1):
    m = h.shape[0]
    sample = prevs is not None
    nj = D_FF_PAD // FF_TILE
    row = lambda i, j: (i, 0)
    col = lambda i, j: (0, j)
    in_specs = [
        pl.BlockSpec((tm, D_MODEL), row, pipeline_mode=pl.Buffered(1)),
        pl.BlockSpec((1, D_MODEL), lambda i, j: (0, 0)),
        pl.BlockSpec((None, D_MODEL, FF_TILE), lambda i, j: (layer, 0, j)),
        pl.BlockSpec((None, D_MODEL, FF_TILE), lambda i, j: (layer, 0, j)),
        pl.BlockSpec((3, FF_TILE), col),
        pl.BlockSpec((3, FF_TILE), col),
        pl.BlockSpec((None, FF_TILE, D_MODEL), lambda i, j: (layer, j, 0)),
    ]
    args = [h, g, wg, wv, cwg, cwv, wd]
    up_rows = tm if sample else SUBLANE
    up_spec = pl.BlockSpec((up_rows, FF_TILE), lambda i, j: (i, j))
    up_shape = jax.ShapeDtypeStruct((m // tm * up_rows, D_FF_PAD), F32)
    out_specs = [pl.BlockSpec((tm, D_MODEL), row), up_spec, up_spec]
    out_shape = [jax.ShapeDtypeStruct((m, D_MODEL), F32), up_shape, up_shape]
    scratch = [pltpu.VMEM((tm, D_MODEL), BF16), pltpu.VMEM((tm + SUBLANE, FF_TILE), F32)]
    if sample:
        in_specs += [pl.BlockSpec((tm, FF_TILE), col)] * 4
        args += list(prevs)
    else:
        scratch.append(pltpu.VMEM((2, nj, SUBLANE, FF_TILE), F32))
    return pl.pallas_call(
        functools.partial(_ffn_kernel, sample=sample, seq_len=seq_len),
        grid=(m // tm, nj),
        in_specs=in_specs,
        out_specs=out_specs,
        out_shape=out_shape,
        scratch_shapes=scratch,
        compiler_params=_cparams("arbitrary", "arbitrary"),
        name="ffn_sample" if sample else "ffn_prompt",
    )(*args)


def _ple_kernel(h_ref, g_ref, p_ref, wpg_ref, wpp_ref, gf_ref, out_ref, *, final):
    h = h_ref[...]
    xn = _rms_rows(h, g_ref[...]).astype(BF16)
    gate = jnp.dot(xn, wpg_ref[...], preferred_element_type=F32)
    gate = 1.0 / (1.0 + jnp.exp(-gate))
    proj = jnp.dot(p_ref[...].astype(BF16), wpp_ref[...], preferred_element_type=F32)
    hn = h + gate * proj
    if final:
        hn = _rms_rows(hn, gf_ref[...])
    out_ref[...] = hn


def _ple(h, g, p, wpg, wpp, g_final, layer, tm, final):
    m = h.shape[0]
    row = lambda i: (i, 0)
    fixed = lambda i: (0, 0)
    return pl.pallas_call(
        functools.partial(_ple_kernel, final=final),
        grid=(m // tm,),
        in_specs=[
            pl.BlockSpec((tm, D_MODEL), row),
            pl.BlockSpec((1, D_MODEL), fixed),
            pl.BlockSpec((tm, PLE_DIM), row),
            pl.BlockSpec((None, D_MODEL, D_MODEL), lambda i: (layer, 0, 0)),
            pl.BlockSpec((None, PLE_DIM, D_MODEL), lambda i: (layer, 0, 0)),
            pl.BlockSpec((1, D_MODEL), fixed),
        ],
        out_specs=pl.BlockSpec((tm, D_MODEL), row),
        out_shape=jax.ShapeDtypeStruct((m, D_MODEL), F32),
        compiler_params=_cparams("arbitrary"),
        name="ple",
    )(h, g, p, wpg, wpp, g_final)


def _expand_state(state, seq_len):
    b, _, c = state.shape
    zeros = jnp.zeros((b, seq_len - 1, c), state.dtype)
    p1 = jnp.concatenate([state[:, 1:2], zeros], axis=1).reshape(b * seq_len, c)
    p2 = jnp.concatenate([state[:, 0:1], state[:, 1:2], zeros[:, 1:]], axis=1).reshape(b * seq_len, c)
    return p1, p2


def _pad_ff(x):
    return jnp.pad(x, ((0, 0), (0, D_FF_PAD - D_FF)))


def kernel(x_prompt, x_sample, cache_k, cache_v, state_conv_mix, state_conv_ffn, page_table,
           p_prompt, p_sample, g_mix, w_in, b_sb, conv_mix_w, g_v, w_s, b_s, g_out, w_o,
           g_ffn, w_up, conv_ffn_w, w_down, g_ple, w_ple_gate, w_ple_proj, g_final):
    depth = w_in.shape[0]
    bp, seq, _ = x_prompt.shape
    n_seq, n_q, _ = x_sample.shape
    assert bp == 1 and n_q >= 2 and n_q * n_seq == CHUNK
    ms = n_seq * n_q
    hp = x_prompt.reshape(seq, D_MODEL)
    hs = x_sample.reshape(ms, D_MODEL)
    gfin = g_final.reshape(1, D_MODEL)
    tril = jnp.tril(jnp.ones((CHUNK, CHUNK), F32))

    ff_pad = ((0, 0), (0, 0), (0, D_FF_PAD - D_FF))
    w_in_b = w_in.astype(BF16)
    w_o_b = w_o.astype(BF16)
    wg_b = jnp.pad(w_up[:, :, :D_FF], ff_pad).astype(BF16)
    wv_b = jnp.pad(w_up[:, :, D_FF:], ff_pad).astype(BF16)
    wd_b = jnp.pad(w_down, ((0, 0), (0, D_FF_PAD - D_FF), (0, 0))).astype(BF16)
    wpg_b = w_ple_gate.astype(BF16)
    wpp_b = w_ple_proj.astype(BF16)

    outs = {k: [] for k in ("kp", "vp", "cbp", "cfp", "ks", "vs", "cbs", "cfs", "cvs")}
    for l in range(depth):
        cwg = _pad_ff(conv_ffn_w[l][:, :D_FF])
        cwv = _pad_ff(conv_ffn_w[l][:, D_FF:])
        gm = g_mix[l].reshape(1, D_MODEL)
        gv = g_v[l].reshape(1, W_C)
        go = g_out[l].reshape(1, D_MODEL)
        gf = g_ffn[l].reshape(1, D_MODEL)
        gp = g_ple[l].reshape(1, D_MODEL)
        final = l == depth - 1

        ws_p = (w_s[l] * tril).astype(BF16)
        bs_p = jnp.repeat(b_s[l].T, HEAD_DIM, axis=1)
        wm_q = (w_s[l] * tril)[:, :n_q, :n_q]
        ws_s = jnp.einsum("ab,hts->hatbs", jnp.eye(n_seq, dtype=F32), wm_q).reshape(H_C, ms, ms).astype(BF16)
        bs_s = jnp.repeat(jnp.tile(b_s[l][:, :n_q].T, (n_seq, 1)), HEAD_DIM, axis=1)

        qkv, kp, vp, rest = _in_proj(hp, gm, w_in_b, l, 1024, head_rows=True)
        qkv_s, ks, vs, rest_s = _in_proj(hs, gm, w_in_b, l, ms, head_rows=False)
        ya, ya_s = _attention_fused(qkv, b_sb[l], qkv_s, ks, vs, cache_k, cache_v, page_table, l)
        hp, xcp = _mix(hp, ya, rest, conv_mix_w[l], gv, ws_p, bs_p, go, w_o_b, l, 256)
        hp, cfg, cfv = _ffn(hp, gf, wg_b, wv_b, cwg, cwv, wd_b, l, 1024)
        hp = _ple(hp, gp, p_prompt[l].reshape(seq, PLE_DIM), wpg_b, wpp_b, gfin, l, 512, final)
        outs["kp"].append(kp.reshape(1, seq, H_A, HEAD_DIM))
        outs["vp"].append(vp.reshape(1, seq, H_A, HEAD_DIM))
        outs["cbp"].append(xcp[seq - 2:].reshape(1, 2, W_B))
        outs["cfp"].append(jnp.concatenate([cfg[-2:, :D_FF], cfv[-2:, :D_FF]],
                                           axis=-1).reshape(1, 2, 2 * D_FF))

        p1m, p2m = _expand_state(state_conv_mix[l], n_q)
        hs, xcs, vvs = _mix(hs, ya_s, rest_s, conv_mix_w[l], gv, ws_s, bs_s, go, w_o_b, l, ms,
                            p1=p1m, p2=p2m, seq_len=n_q)
        p1g, p2g = _expand_state(_pad_ff(state_conv_ffn[l][..., :D_FF].reshape(n_seq * 2, D_FF))
                                 .reshape(n_seq, 2, D_FF_PAD), n_q)
        p1v, p2v = _expand_state(_pad_ff(state_conv_ffn[l][..., D_FF:].reshape(n_seq * 2, D_FF))
                                 .reshape(n_seq, 2, D_FF_PAD), n_q)
        hs, upg, upv = _ffn(hs, gf, wg_b, wv_b, cwg, cwv, wd_b, l, ms,
                            prevs=(p1g, p2g, p1v, p2v), seq_len=n_q)
        hs = _ple(hs, gp, p_sample[l].reshape(ms, PLE_DIM), wpg_b, wpp_b, gfin, l, ms, final)
        outs["ks"].append(ks.reshape(n_seq, n_q, H_A, HEAD_DIM))
        outs["vs"].append(vs.reshape(n_seq, n_q, H_A, HEAD_DIM))
        outs["cbs"].append(xcs.reshape(n_seq, n_q, W_B)[:, n_q - 2:])
        up_s = jnp.concatenate([upg[:, :D_FF], upv[:, :D_FF]], axis=-1)
        outs["cfs"].append(up_s.reshape(n_seq, n_q, 2 * D_FF)[:, n_q - 2:])
        outs["cvs"].append(vvs.reshape(n_seq, n_q, W_C))

    st = lambda k: jnp.stack(outs[k])
    return (hp.reshape(1, seq, D_MODEL), hs.reshape(n_seq, n_q, D_MODEL),
            st("kp"), st("vp"), st("cbp"), st("cfp"),
            st("ks"), st("vs"), st("cbs"), st("cfs"), st("cvs"))
```
